```python
import jax, jax.numpy as jnp
from jax import lax
import numpy as np

D_MODEL = 1024
BATCH = 8
SEQ = 8192
DEPTH = 1
DEC_BATCH = 8
DEC_SEQ = 64
PAST_LEN = 4096

CHUNK = 64
N_HEADS = 8
N_KV_HEADS = 2
GROUP = N_HEADS // N_KV_HEADS
HEAD_DIM = 64
ATTN_WIDTH = N_HEADS * HEAD_DIM
KV_WIDTH = N_KV_HEADS * HEAD_DIM
WINDOW = 128
WINDOW_CHUNKS = WINDOW // CHUNK
ROT_DIM = HEAD_DIM // 4
ROPE_THETA = 500000.0
LRU_WIDTH = D_MODEL - ATTN_WIDTH
LRU_BLOCKS = 8
LRU_BLOCK = LRU_WIDTH // LRU_BLOCKS
CONV_WIDTH = 4
LRU_C = 8.0
MIX_WIDTH = ATTN_WIDTH + LRU_WIDTH
IN_COLS = ATTN_WIDTH + 2 * KV_WIDTH + 2 * LRU_WIDTH
D_FF = 2816
PLE_DIM = 256
ALPHA = (2.0 * DEPTH) ** 0.25
BETA = (8.0 * DEPTH) ** -0.25
LN_EPS = 1e-5
NEG_INF = -1e30

kernel_name = 'hybrid_streaming_swa_rglru_step'


def _layer_norm(x, g, b):
    xf = x.astype(jnp.float32)
    mu = jnp.mean(xf, -1, keepdims=True)
    var = jnp.mean(jnp.square(xf - mu), -1, keepdims=True)
    y = (xf - mu) * lax.rsqrt(var + LN_EPS)
    return (y * g.astype(jnp.float32) + b.astype(jnp.float32)).astype(x.dtype)


def _swiglu(x, wg, wu, wd):
    return (jax.nn.silu(x @ wg) * (x @ wu)) @ wd


def _rope(x, pos):
    half = ROT_DIM // 2
    inv = jnp.power(jnp.float32(ROPE_THETA), -jnp.arange(half, dtype=jnp.float32) * (2.0 / ROT_DIM))
    ang = pos.astype(jnp.float32)[:, None] * inv[None, :]
    cos = jnp.cos(ang)[None, :, None, :].astype(x.dtype)
    sin = jnp.sin(ang)[None, :, None, :].astype(x.dtype)
    x1 = x[..., :half]
    x2 = x[..., half:ROT_DIM]
    return jnp.concatenate([x1 * cos - x2 * sin, x2 * cos + x1 * sin, x[..., ROT_DIM:]], axis=-1)


def _sink_attention(q, kb, vb, sinks, valid):
    s = jnp.einsum('bncvgd,bnkvd->bnvgck', q, kb).astype(jnp.float32) * (HEAD_DIM ** -0.5)
    if valid is not None:
        s = jnp.where(valid[None, :, None, None, None, :], s, NEG_INF)
    sink = sinks.astype(jnp.float32).reshape(N_KV_HEADS, GROUP)[None, None, :, :, None, None]
    m = jnp.maximum(jnp.max(s, axis=-1, keepdims=True), sink)
    pr = jnp.exp(s - m)
    denom = jnp.sum(pr, axis=-1, keepdims=True) + jnp.exp(sink - m)
    w = (pr / denom).astype(vb.dtype)
    return jnp.einsum('bnvgck,bnkvd->bncvgd', w, vb)


def _prompt_attention(q, k, v, sinks):
    B, S = q.shape[0], q.shape[1]
    nc = S // CHUNK
    qc = q.reshape(B, nc, CHUNK, N_KV_HEADS, GROUP, HEAD_DIM)

    def band(t):
        tc = t.reshape(B, nc, CHUNK, N_KV_HEADS, HEAD_DIM)
        tp = jnp.pad(tc, ((0, 0), (WINDOW_CHUNKS, 0), (0, 0), (0, 0), (0, 0)))
        return jnp.concatenate([tp[:, j:j + nc] for j in range(WINDOW_CHUNKS + 1)], axis=2)

    key_chunk = jnp.arange(nc)[:, None] - WINDOW_CHUNKS + (jnp.arange((WINDOW_CHUNKS + 1) * CHUNK) // CHUNK)[None, :]
    o = _sink_attention(qc, band(k), band(v), sinks, key_chunk >= 0)
    return o.reshape(B, S, ATTN_WIDTH), k[:, -WINDOW:], v[:, -WINDOW:]


def _sample_attention(q, k, v, k_past, v_past, sinks):
    B, T = q.shape[0], q.shape[1]
    k_all = jnp.concatenate([k_past.astype(k.dtype), k], axis=1)
    v_all = jnp.concatenate([v_past.astype(v.dtype), v], axis=1)
    qc = q.reshape(B, 1, T, N_KV_HEADS, GROUP, HEAD_DIM)
    o = _sink_attention(qc, k_all[:, None], v_all[:, None], sinks, None)
    return o.reshape(B, T, ATTN_WIDTH), k_all[:, -WINDOW:], v_all[:, -WINDOW:]


def _lin_combine(c1, c2):
    a1, b1 = c1
    a2, b2 = c2
    return a1 * a2, a2 * b1 + b2


def _rglru(xb, conv_prev, h_prev, conv_w, conv_b, w_a, b_a, w_x, b_x, lam):
    B, T = xb.shape[0], xb.shape[1]
    xp = jnp.concatenate([conv_prev.astype(xb.dtype), xb], axis=1)
    new_conv = xp[:, -(CONV_WIDTH - 1):]
    xc = conv_b + conv_w[0] * xp[:, 0:T]
    for j in range(1, CONV_WIDTH):
        xc = xc + conv_w[j] * xp[:, j:j + T]
    xf = xc.astype(jnp.float32)
    xr = xf.reshape(B, T, LRU_BLOCKS, LRU_BLOCK)
    r = jax.nn.sigmoid(jnp.einsum('btnc,ncd->btnd', xr, w_a.astype(jnp.float32)).reshape(B, T, LRU_WIDTH) + b_a.astype(jnp.float32))
    i = jax.nn.sigmoid(jnp.einsum('btnc,ncd->btnd', xr, w_x.astype(jnp.float32)).reshape(B, T, LRU_WIDTH) + b_x.astype(jnp.float32))
    log_a = -LRU_C * jax.nn.softplus(-lam.astype(jnp.float32)) * r
    a = jnp.exp(log_a)
    u = jnp.sqrt(-jnp.expm1(2.0 * log_a)) * (i * xf)
    u = u.at[:, 0].add(a[:, 0] * h_prev.astype(jnp.float32))
    _, hs = lax.associative_scan(_lin_combine, (a, u), axis=1)
    return hs, new_conv, hs[:, -1]


def _layer(x, p, pos, k_past, v_past, conv_prev, h_prev, prm):
    B, T = x.shape[0], x.shape[1]
    h = _layer_norm(ALPHA * x + 0.5 * _swiglu(x, prm['ffn1_wg'], prm['ffn1_wu'], prm['ffn1_wd']), prm['ln1_g'], prm['ln1_b'])
    z = h @ prm['w_in']
    o1 = ATTN_WIDTH
    o2 = o1 + KV_WIDTH
    o3 = o2 + KV_WIDTH
    o4 = o3 + LRU_WIDTH
    q = _rope(z[..., :o1].reshape(B, T, N_HEADS, HEAD_DIM), pos)
    k = _rope(z[..., o1:o2].reshape(B, T, N_KV_HEADS, HEAD_DIM), pos)
    v = z[..., o2:o3].reshape(B, T, N_KV_HEADS, HEAD_DIM)
    xb = z[..., o3:o4]
    gb = z[..., o4:]
    if k_past is None:
        attn, new_k, new_v = _prompt_attention(q, k, v, prm['attn_sinks'])
    else:
        attn, new_k, new_v = _sample_attention(q, k, v, k_past, v_past, prm['attn_sinks'])
    hs, new_conv, new_h = _rglru(xb, conv_prev, h_prev, prm['conv_w'], prm['conv_b'], prm['lru_wa'], prm['lru_ba'], prm['lru_wx'], prm['lru_bx'], prm['lru_lambda'])
    lru = (hs * jax.nn.gelu(gb.astype(jnp.float32))).astype(x.dtype)
    mix = jnp.concatenate([attn, lru], axis=-1) @ prm['w_out']
    h = _layer_norm(ALPHA * h + mix, prm['ln2_g'], prm['ln2_b'])
    h = _layer_norm(ALPHA * h + 0.5 * _swiglu(h, prm['ffn2_wg'], prm['ffn2_wu'], prm['ffn2_wd']), prm['ln3_g'], prm['ln3_b'])
    y = h + jax.nn.sigmoid(h @ prm['w_ple_gate']) * (p @ prm['w_ple'])
    return y, new_k, new_v, new_conv, new_h


def setup_inputs(seed: int = 0) -> dict:
    key = jax.random.key(seed)
    ks = jax.random.split(key, 40)
    f32 = jnp.float32
    L = DEPTH
    D = D_MODEL

    def nrm(k, shape, scale):
        return jax.random.normal(k, shape, f32) * scale

    u = jax.random.uniform(ks[30], (L, LRU_WIDTH), f32, 0.9, 0.999)
    s = u ** (1.0 / LRU_C)
    lam = jnp.log(s) - jnp.log1p(-s)
    return {
        'x_prompt': nrm(ks[0], (BATCH, SEQ, D), 1.0),
        'x_sample': nrm(ks[1], (DEC_BATCH, DEC_SEQ, D), 1.0),
        'p_prompt': nrm(ks[2], (L, BATCH, SEQ, PLE_DIM), 1.0),
        'p_sample': nrm(ks[3], (L, DEC_BATCH, DEC_SEQ, PLE_DIM), 1.0),
        'cache_k': nrm(ks[4], (L, DEC_BATCH, WINDOW, N_KV_HEADS, HEAD_DIM), 1.0),
        'cache_v': nrm(ks[5], (L, DEC_BATCH, WINDOW, N_KV_HEADS, HEAD_DIM), 1.0),
        'state_conv': nrm(ks[6], (L, DEC_BATCH, CONV_WIDTH - 1, LRU_WIDTH), 1.0),
        'state_h': nrm(ks[7], (L, DEC_BATCH, LRU_WIDTH), 0.5),
        'ffn1_wg': nrm(ks[8], (L, D, D_FF), D ** -0.5),
        'ffn1_wu': nrm(ks[9], (L, D, D_FF), D ** -0.5),
        'ffn1_wd': nrm(ks[10], (L, D_FF, D), BETA * D_FF ** -0.5),
        'ln1_g': 1.0 + nrm(ks[11], (L, D), 0.02),
        'ln1_b': nrm(ks[12], (L, D), 0.02),
        'w_in': nrm(ks[13], (L, D, IN_COLS), D ** -0.5),
        'attn_sinks': nrm(ks[14], (L, N_HEADS), 0.5),
        'conv_w': nrm(ks[15], (L, CONV_WIDTH, LRU_WIDTH), CONV_WIDTH ** -0.5),
        'conv_b': nrm(ks[16], (L, LRU_WIDTH), 0.02),
        'lru_wa': nrm(ks[17], (L, LRU_BLOCKS, LRU_BLOCK, LRU_BLOCK), LRU_BLOCK ** -0.5),
        'lru_ba': nrm(ks[18], (L, LRU_WIDTH), 0.02),
        'lru_wx': nrm(ks[19], (L, LRU_BLOCKS, LRU_BLOCK, LRU_BLOCK), LRU_BLOCK ** -0.5),
        'lru_bx': nrm(ks[20], (L, LRU_WIDTH), 0.02),
        'lru_lambda': lam,
        'w_out': nrm(ks[21], (L, MIX_WIDTH, D), BETA * MIX_WIDTH ** -0.5),
        'ln2_g': 1.0 + nrm(ks[22], (L, D), 0.02),
        'ln2_b': nrm(ks[23], (L, D), 0.02),
        'ffn2_wg': nrm(ks[24], (L, D, D_FF), D ** -0.5),
        'ffn2_wu': nrm(ks[25], (L, D, D_FF), D ** -0.5),
        'ffn2_wd': nrm(ks[26], (L, D_FF, D), BETA * D_FF ** -0.5),
        'ln3_g': 1.0 + nrm(ks[27], (L, D), 0.02),
        'ln3_b': nrm(ks[28], (L, D), 0.02),
        'w_ple': nrm(ks[29], (L, PLE_DIM, D), PLE_DIM ** -0.5),
        'w_ple_gate': nrm(ks[31], (L, D, D), D ** -0.5),
    }


def reference(x_prompt, x_sample, p_prompt, p_sample, cache_k, cache_v, state_conv, state_h,
              ffn1_wg, ffn1_wu, ffn1_wd, ln1_g, ln1_b, w_in, attn_sinks, conv_w, conv_b,
              lru_wa, lru_ba, lru_wx, lru_bx, lru_lambda, w_out, ln2_g, ln2_b,
              ffn2_wg, ffn2_wu, ffn2_wd, ln3_g, ln3_b, w_ple, w_ple_gate):
    pos_prompt = jnp.arange(x_prompt.shape[1])
    pos_sample = PAST_LEN + jnp.arange(x_sample.shape[1])
    yp, ys = x_prompt, x_sample
    kp_l, vp_l, cp_l, hp_l = [], [], [], []
    ks_l, vs_l, cs_l, hs_l = [], [], [], []
    for i in range(DEPTH):
        prm = dict(ffn1_wg=ffn1_wg[i], ffn1_wu=ffn1_wu[i], ffn1_wd=ffn1_wd[i], ln1_g=ln1_g[i], ln1_b=ln1_b[i],
                   w_in=w_in[i], attn_sinks=attn_sinks[i], conv_w=conv_w[i], conv_b=conv_b[i],
                   lru_wa=lru_wa[i], lru_ba=lru_ba[i], lru_wx=lru_wx[i], lru_bx=lru_bx[i], lru_lambda=lru_lambda[i],
                   w_out=w_out[i], ln2_g=ln2_g[i], ln2_b=ln2_b[i], ffn2_wg=ffn2_wg[i], ffn2_wu=ffn2_wu[i],
                   ffn2_wd=ffn2_wd[i], ln3_g=ln3_g[i], ln3_b=ln3_b[i], w_ple=w_ple[i], w_ple_gate=w_ple_gate[i])
        bp = yp.shape[0]
        conv0 = jnp.zeros((bp, CONV_WIDTH - 1, LRU_WIDTH), yp.dtype)
        h0 = jnp.zeros((bp, LRU_WIDTH), jnp.float32)
        yp, kp, vp, cp, hp = _layer(yp, p_prompt[i], pos_prompt, None, None, conv0, h0, prm)
        ys, kn, vn, cn, hn = _layer(ys, p_sample[i], pos_sample, cache_k[i], cache_v[i], state_conv[i], state_h[i], prm)
        kp_l.append(kp); vp_l.append(vp); cp_l.append(cp); hp_l.append(hp)
        ks_l.append(kn); vs_l.append(vn); cs_l.append(cn); hs_l.append(hn)
    return (yp, ys, jnp.stack(kp_l), jnp.stack(vp_l), jnp.stack(cp_l), jnp.stack(hp_l),
            jnp.stack(ks_l), jnp.stack(vs_l), jnp.stack(cs_l), jnp.stack(hs_l))
```

```python
import functools

import jax
import jax.numpy as jnp
from jax import lax
from jax.experimental import pallas as pl
from jax.experimental.pallas import tpu as pltpu

D_MODEL = 1024
CHUNK = 64
N_HEADS = 8
N_KV_HEADS = 2
GROUP = N_HEADS // N_KV_HEADS
HEAD_DIM = 64
ATTN_WIDTH = N_HEADS * HEAD_DIM
KV_WIDTH = N_KV_HEADS * HEAD_DIM
WINDOW = 128
ROT_DIM = HEAD_DIM // 4
ROPE_THETA = 500000.0
LRU_WIDTH = D_MODEL - ATTN_WIDTH
LRU_BLOCKS = 8
LRU_BLOCK = LRU_WIDTH // LRU_BLOCKS
CONV_WIDTH = 4
LRU_C = 8.0
IN_COLS = ATTN_WIDTH + 2 * KV_WIDTH + 2 * LRU_WIDTH
D_FF = 2816
PLE_DIM = 256
DEPTH = 1
PAST_LEN = 4096
ALPHA = (2.0 * DEPTH) ** 0.25
LN_EPS = 1e-5
NEG_INF = -1e30

V7X_LANES = 128
V7X_SUBLANES = 8
V7X_MXU_COLS = 256
V7X_VMEM_BYTES = 64 * 1024 * 1024
LRU_SLABS = LRU_WIDTH // V7X_LANES

_HEAD_SLOTS = tuple(h for j in range(GROUP) for h in (j, GROUP + j))

_BF16 = jnp.bfloat16
_F32 = jnp.float32


def _ff_chunks():
    step = 4 * V7X_MXU_COLS
    return tuple((lo, min(lo + step, D_FF)) for lo in range(0, D_FF, step))


def _layer_norm(y, g, b):
    mu = jnp.mean(y, -1, keepdims=True)
    var = jnp.mean(jnp.square(y - mu), -1, keepdims=True)
    return (y - mu) * lax.rsqrt(var + LN_EPS) * g + b


def _macaron(x, wg_ref, wu_ref, wd_ref, g, b):
    x16 = x.astype(_BF16)
    acc = None
    for lo, hi in _ff_chunks():
        gate = jnp.dot(x16, wg_ref[:, lo:hi], preferred_element_type=_F32)
        up = jnp.dot(x16, wu_ref[:, lo:hi], preferred_element_type=_F32)
        act = (jax.nn.silu(gate) * up).astype(_BF16)
        part = jnp.dot(act, wd_ref[lo:hi, :], preferred_element_type=_F32)
        acc = part if acc is None else acc + part
    return _layer_norm(ALPHA * x + 0.5 * acc, g, b)


def _rope_slab(xs, cos, sin_signed, low_half):
    up = pltpu.roll(xs, V7X_LANES - ROT_DIM // 2, axis=1)
    dn = pltpu.roll(xs, ROT_DIM // 2, axis=1)
    return xs * cos + jnp.where(low_half, up, dn) * sin_signed


def _ffn_in_kernel(x_ref, cos_ref, sin_ref, wg_ref, wu_ref, wd_ref, g_ref, b_ref, win_ref,
                   h_ref, q_ref, k_ref, v_ref, xb_ref, gb_ref):
    h = _macaron(x_ref[...], wg_ref, wu_ref, wd_ref, g_ref[...], b_ref[...])
    h_ref[...] = h
    z = jnp.dot(h.astype(_BF16), win_ref[...], preferred_element_type=_F32)
    cos = cos_ref[...]
    sin = sin_ref[...]
    lane = lax.broadcasted_iota(jnp.int32, cos.shape, 1)
    low_half = (lane & (HEAD_DIM - 1)) < (ROT_DIM // 2)
    for j in range(ATTN_WIDTH // V7X_LANES):
        sl = slice(j * V7X_LANES, (j + 1) * V7X_LANES)
        q_ref[:, sl] = _rope_slab(z[:, sl], cos, sin, low_half)
    o1 = ATTN_WIDTH
    o2 = o1 + KV_WIDTH
    o3 = o2 + KV_WIDTH
    o4 = o3 + LRU_WIDTH
    k_ref[...] = _rope_slab(z[:, o1:o2], cos, sin, low_half)
    v_ref[...] = z[:, o2:o3]
    xb_ref[...] = z[:, o3:o4]
    gb_ref[...] = z[:, o4:]


def _attention_chunk(q16, kc, vc, sinks_ref, masked_keys):
    lane = lax.broadcasted_iota(jnp.int32, (CHUNK, V7X_LANES), 1)
    outs = []
    for vh in range(N_KV_HEADS):
        own = (lane >= HEAD_DIM) if vh else (lane < HEAD_DIM)
        qs = [jnp.where(own, q16[:, j * V7X_LANES:(j + 1) * V7X_LANES], jnp.zeros((), _BF16))
              for j in range(GROUP)]
        q4 = jnp.concatenate(qs, axis=0)
        s = lax.dot_general(q4, kc, (((1,), (1,)), ((), ())),
                            preferred_element_type=_F32) * (HEAD_DIM ** -0.5)
        if masked_keys is not None:
            s = jnp.where(masked_keys, NEG_INF, s)
        sink = jnp.concatenate(
            [jnp.full((CHUNK, 1), sinks_ref[vh * GROUP + g], _F32) for g in range(GROUP)], axis=0)
        m = jnp.maximum(jnp.max(s, axis=-1, keepdims=True), sink)
        pr = jnp.exp(s - m)
        denom = jnp.sum(pr, axis=-1, keepdims=True) + jnp.exp(sink - m)
        w = (pr * (1.0 / denom)).astype(_BF16)
        outs.append(jnp.dot(w, vc, preferred_element_type=_F32))
    low = lane < HEAD_DIM
    slabs = [jnp.where(low, outs[0][g * CHUNK:(g + 1) * CHUNK], outs[1][g * CHUNK:(g + 1) * CHUNK])
             for g in range(GROUP)]
    return jnp.concatenate(slabs, axis=1)


def _lru_scan(a_buf, u_buf, loc_buf, cum_buf, cin_buf, hs_buf, h0, rows):
    groups = rows // V7X_SUBLANES
    lasts = []
    for s in range(LRU_SLABS):
        def slab(ref, r):
            return ref.at[s, pl.ds(r, groups, stride=V7X_SUBLANES), :]

        loc = slab(u_buf, 0)[...]
        cum = slab(a_buf, 0)[...]
        slab(loc_buf, 0)[...] = loc
        slab(cum_buf, 0)[...] = cum
        for r in range(1, V7X_SUBLANES):
            ar = slab(a_buf, r)[...]
            loc = ar * loc + slab(u_buf, r)[...]
            cum = ar * cum
            slab(loc_buf, r)[...] = loc
            slab(cum_buf, r)[...] = cum
        carry = h0[:, s * V7X_LANES:(s + 1) * V7X_LANES]
        for gidx in range(groups):
            cin_buf[s, gidx:gidx + 1, :] = carry
            carry = cum[gidx:gidx + 1, :] * carry + loc[gidx:gidx + 1, :]
        cin = cin_buf[s, 0:groups, :]
        for r in range(V7X_SUBLANES):
            slab(hs_buf, r)[...] = slab(cum_buf, r)[...] * cin + slab(loc_buf, r)[...]
        lasts.append(carry)
    return jnp.concatenate(lasts, axis=1)


def _mixer_kernel(sinks_ref, h1_ref, q_ref, k_ref, v_ref, xb_ref, gb_ref,
                  k0_ref, v0_ref, conv0_ref, hst0_ref,
                  convw_ref, convb_ref, wax_ref, ba_ref, bx_ref, lam_ref,
                  wout_ref, g_ref, b_ref,
                  h2_ref, hst_ref,
                  kbuf, vbuf, xbuf, a_buf, u_buf, loc_buf, cum_buf, cin_buf, hs_buf, hcar,
                  *, rows, mask_initial_keys):
    t = pl.program_id(1)
    conv_pad = V7X_SUBLANES

    @pl.when(t == 0)
    def _():
        kbuf[0:WINDOW, :] = k0_ref[...].astype(_BF16)
        vbuf[0:WINDOW, :] = v0_ref[...].astype(_BF16)
        xbuf[0:conv_pad, :] = conv0_ref[...]
        hcar[...] = hst0_ref[...]

    kbuf[WINDOW:WINDOW + rows, :] = k_ref[...].astype(_BF16)
    vbuf[WINDOW:WINDOW + rows, :] = v_ref[...].astype(_BF16)
    xb = xb_ref[...]
    xbuf[conv_pad:conv_pad + rows, :] = xb

    q16 = q_ref[...].astype(_BF16)
    key_idx = lax.broadcasted_iota(jnp.int32, (GROUP * CHUNK, WINDOW + CHUNK), 1)
    attn_chunks = []
    for c in range(rows // CHUNK):
        kc = kbuf[c * CHUNK:c * CHUNK + WINDOW + CHUNK, :]
        vc = vbuf[c * CHUNK:c * CHUNK + WINDOW + CHUNK, :]
        masked = None
        if mask_initial_keys and c * CHUNK < WINDOW:
            masked = jnp.logical_and(t == 0, key_idx < WINDOW - c * CHUNK)
        attn_chunks.append(_attention_chunk(q16[c * CHUNK:(c + 1) * CHUNK, :], kc, vc, sinks_ref, masked))
    attn = jnp.concatenate(attn_chunks, axis=0)

    cw = convw_ref[...]
    xc = convb_ref[...] + cw[0:1, :] * xbuf[conv_pad - 3:conv_pad - 3 + rows, :]
    for j in range(1, CONV_WIDTH):
        xc = xc + cw[j:j + 1, :] * xbuf[conv_pad - 3 + j:conv_pad - 3 + j + rows, :]
    gates = jnp.dot(xc.astype(_BF16), wax_ref[...], preferred_element_type=_F32)
    r_gate = jax.nn.sigmoid(gates[:, :LRU_WIDTH] + ba_ref[...])
    i_gate = jax.nn.sigmoid(gates[:, LRU_WIDTH:] + bx_ref[...])
    log_a = (-LRU_C * jax.nn.softplus(-lam_ref[...])) * r_gate
    a_gate = jnp.exp(log_a)
    for s in range(LRU_SLABS):
        a_buf[s] = a_gate[:, s * V7X_LANES:(s + 1) * V7X_LANES]
    one_minus_a2 = -jnp.tanh(log_a) * (a_gate * a_gate + 1.0)
    u_val = jnp.sqrt(one_minus_a2) * (i_gate * xc)
    for s in range(LRU_SLABS):
        u_buf[s] = u_val[:, s * V7X_LANES:(s + 1) * V7X_LANES]
    last = _lru_scan(a_buf, u_buf, loc_buf, cum_buf, cin_buf, hs_buf, hcar[...], rows)
    hcar[...] = last
    hst_ref[...] = last
    hs = jnp.concatenate([hs_buf[s] for s in range(LRU_SLABS)], axis=1)
    lru = hs * jax.nn.gelu(gb_ref[...])

    mix = jnp.dot(jnp.concatenate([attn, lru], axis=1).astype(_BF16), wout_ref[...],
                  preferred_element_type=_F32)
    h2_ref[...] = _layer_norm(ALPHA * h1_ref[...] + mix, g_ref[...], b_ref[...])

    kbuf[0:WINDOW, :] = kbuf[rows:rows + WINDOW, :]
    vbuf[0:WINDOW, :] = vbuf[rows:rows + WINDOW, :]
    xbuf[0:conv_pad, :] = xbuf[rows:rows + conv_pad, :]


def _ffn_out_kernel(h_ref, p_ref, wg_ref, wu_ref, wd_ref, g_ref, b_ref, wgate_ref, wple_ref, y_ref):
    h = _macaron(h_ref[...], wg_ref, wu_ref, wd_ref, g_ref[...], b_ref[...])
    gate = jax.nn.sigmoid(jnp.dot(h.astype(_BF16), wgate_ref[...], preferred_element_type=_F32))
    ple = jnp.dot(p_ref[...].astype(_BF16), wple_ref[...], preferred_element_type=_F32)
    y_ref[...] = h + gate * ple


def _resident(shape):
    zeros = (0,) * len(shape)
    return pl.BlockSpec(shape, lambda *_: zeros, pipeline_mode=pl.Buffered(1))


def _row_tile(total_rows):
    return min(512, total_rows)


def _vmem_limit():
    return V7X_VMEM_BYTES - 6 * 1024 * 1024


def _ffn_in(x2d, cos_tab, sin_tab, w):
    n = x2d.shape[0]
    tm = _row_tile(n)
    tab_tiles = cos_tab.shape[0] // tm
    row = lambda width: pl.BlockSpec((tm, width), lambda i: (i, 0))
    tab = pl.BlockSpec((tm, V7X_LANES), lambda i: (i % tab_tiles, 0))
    widths = (D_MODEL, ATTN_WIDTH, KV_WIDTH, KV_WIDTH, LRU_WIDTH, LRU_WIDTH)
    return pl.pallas_call(
        _ffn_in_kernel,
        grid=(n // tm,),
        in_specs=[row(D_MODEL), tab, tab,
                  _resident((D_MODEL, D_FF)), _resident((D_MODEL, D_FF)), _resident((D_FF, D_MODEL)),
                  _resident((1, D_MODEL)), _resident((1, D_MODEL)), _resident((D_MODEL, IN_COLS))],
        out_specs=[row(wd) for wd in widths],
        out_shape=[jax.ShapeDtypeStruct((n, wd), _F32) for wd in widths],
        compiler_params=pltpu.CompilerParams(dimension_semantics=("arbitrary",),
                                             vmem_limit_bytes=_vmem_limit()),
        name="ffn_in",
    )(x2d, cos_tab, sin_tab, w["ffn1_wg"], w["ffn1_wu"], w["ffn1_wd"], w["ln1_g"], w["ln1_b"], w["w_in"])


def _mixer(h1, q, k, v, xb, gb, k0, v0, conv0, hst0, w, *, batch, seq, mask_initial_keys):
    tm = _row_tile(seq)
    nt = seq // tm
    row = lambda width: pl.BlockSpec((tm, width), lambda b, t: (b * nt + t, 0))
    per_batch = lambda r, c: pl.BlockSpec((None, r, c), lambda b, t: (b, 0, 0))
    lru_buf = pltpu.VMEM((LRU_SLABS, tm, V7X_LANES), _F32)
    kernel = functools.partial(_mixer_kernel, rows=tm, mask_initial_keys=mask_initial_keys)
    return pl.pallas_call(
        kernel,
        grid=(batch, nt),
        in_specs=[pl.BlockSpec(memory_space=pltpu.SMEM),
                  row(D_MODEL), row(ATTN_WIDTH), row(KV_WIDTH), row(KV_WIDTH), row(LRU_WIDTH), row(LRU_WIDTH),
                  per_batch(WINDOW, KV_WIDTH), per_batch(WINDOW, KV_WIDTH),
                  per_batch(V7X_SUBLANES, LRU_WIDTH), per_batch(1, LRU_WIDTH),
                  _resident((CONV_WIDTH, LRU_WIDTH)), _resident((1, LRU_WIDTH)),
                  _resident((LRU_WIDTH, 2 * LRU_WIDTH)), _resident((1, LRU_WIDTH)), _resident((1, LRU_WIDTH)),
                  _resident((1, LRU_WIDTH)),
                  _resident((D_MODEL, D_MODEL)), _resident((1, D_MODEL)), _resident((1, D_MODEL))],
        out_specs=[row(D_MODEL), per_batch(1, LRU_WIDTH)],
        out_shape=[jax.ShapeDtypeStruct((batch * seq, D_MODEL), _F32),
                   jax.ShapeDtypeStruct((batch, 1, LRU_WIDTH), _F32)],
        scratch_shapes=[pltpu.VMEM((WINDOW + tm, KV_WIDTH), _BF16), pltpu.VMEM((WINDOW + tm, KV_WIDTH), _BF16),
                        pltpu.VMEM((V7X_SUBLANES + tm, LRU_WIDTH), _F32),
                        lru_buf, lru_buf, lru_buf, lru_buf,
                        pltpu.VMEM((LRU_SLABS, max(tm // V7X_SUBLANES, V7X_SUBLANES), V7X_LANES), _F32),
                        lru_buf, pltpu.VMEM((1, LRU_WIDTH), _F32)],
        compiler_params=pltpu.CompilerParams(dimension_semantics=("arbitrary", "arbitrary"),
                                             vmem_limit_bytes=_vmem_limit()),
        name="mixer",
    )(w["attn_sinks"], h1, q, k, v, xb, gb, k0, v0, conv0, hst0,
      w["conv_w"], w["conv_b"], w["lru_wax"], w["lru_ba"], w["lru_bx"], w["lru_lambda"],
      w["w_out"], w["ln2_g"], w["ln2_b"])


def _ffn_out(h2d, p2d, w):
    n = h2d.shape[0]
    tm = _row_tile(n)
    row = lambda width: pl.BlockSpec((tm, width), lambda i: (i, 0))
    return pl.pallas_call(
        _ffn_out_kernel,
        grid=(n // tm,),
        in_specs=[row(D_MODEL), row(PLE_DIM),
                  _resident((D_MODEL, D_FF)), _resident((D_MODEL, D_FF)), _resident((D_FF, D_MODEL)),
                  _resident((1, D_MODEL)), _resident((1, D_MODEL)),
                  _resident((D_MODEL, D_MODEL)), _resident((PLE_DIM, D_MODEL))],
        out_specs=row(D_MODEL),
        out_shape=jax.ShapeDtypeStruct((n, D_MODEL), _F32),
        compiler_params=pltpu.CompilerParams(dimension_semantics=("arbitrary",),
                                             vmem_limit_bytes=_vmem_limit()),
        name="ffn_out",
    )(h2d, p2d, w["ffn2_wg"], w["ffn2_wu"], w["ffn2_wd"], w["ln3_g"], w["ln3_b"], w["w_ple_gate"], w["w_ple"])


def _rope_tables(pos):
    half = ROT_DIM // 2
    inv = jnp.power(jnp.float32(ROPE_THETA), -jnp.arange(half, dtype=_F32) * (2.0 / ROT_DIM))
    ang = pos.astype(_F32)[:, None] * inv[None, :]
    cos, sin = jnp.cos(ang), jnp.sin(ang)
    n = pos.shape[0]
    ident = jnp.ones((n, HEAD_DIM - ROT_DIM), _F32)
    cos_head = jnp.concatenate([cos, cos, ident], axis=1)
    sin_head = jnp.concatenate([-sin, sin, 0.0 * ident], axis=1)
    reps = V7X_LANES // HEAD_DIM
    return jnp.tile(cos_head, (1, reps)), jnp.tile(sin_head, (1, reps))


def _prepare_weights(i, ffn1_wg, ffn1_wu, ffn1_wd, ln1_g, ln1_b, w_in, attn_sinks, conv_w, conv_b,
                     lru_wa, lru_ba, lru_wx, lru_bx, lru_lambda, w_out, ln2_g, ln2_b,
                     ffn2_wg, ffn2_wu, ffn2_wd, ln3_g, ln3_b, w_ple, w_ple_gate):
    row = lambda a: a[i].reshape(1, -1)
    slots = jnp.asarray(_HEAD_SLOTS)
    win = w_in[i]
    wq = win[:, :ATTN_WIDTH].reshape(D_MODEL, N_HEADS, HEAD_DIM)[:, slots].reshape(D_MODEL, ATTN_WIDTH)
    wout = w_out[i]
    wo_attn = wout[:ATTN_WIDTH].reshape(N_HEADS, HEAD_DIM, D_MODEL)[slots].reshape(ATTN_WIDTH, D_MODEL)
    eye = jnp.eye(LRU_BLOCKS, dtype=_F32)
    blockdiag = lambda wb: jnp.einsum("ncd,nm->ncmd", wb, eye).reshape(LRU_WIDTH, LRU_WIDTH)
    return dict(
        ffn1_wg=ffn1_wg[i].astype(_BF16), ffn1_wu=ffn1_wu[i].astype(_BF16), ffn1_wd=ffn1_wd[i].astype(_BF16),
        ln1_g=row(ln1_g), ln1_b=row(ln1_b),
        w_in=jnp.concatenate([wq, win[:, ATTN_WIDTH:]], axis=1).astype(_BF16),
        attn_sinks=attn_sinks[i],
        conv_w=conv_w[i], conv_b=row(conv_b),
        lru_wax=jnp.concatenate([blockdiag(lru_wa[i]), blockdiag(lru_wx[i])], axis=1).astype(_BF16),
        lru_ba=row(lru_ba), lru_bx=row(lru_bx), lru_lambda=row(lru_lambda),
        w_out=jnp.concatenate([wo_attn, wout[ATTN_WIDTH:]], axis=0).astype(_BF16),
        ln2_g=row(ln2_g), ln2_b=row(ln2_b),
        ffn2_wg=ffn2_wg[i].astype(_BF16), ffn2_wu=ffn2_wu[i].astype(_BF16), ffn2_wd=ffn2_wd[i].astype(_BF16),
        ln3_g=row(ln3_g), ln3_b=row(ln3_b),
        w_ple=w_ple[i].astype(_BF16), w_ple_gate=w_ple_gate[i].astype(_BF16))


def _layer(x, p, pos, k_past, v_past, conv_prev, h_prev, w):
    batch, seq = x.shape[0], x.shape[1]
    n = batch * seq
    tm = _row_tile(n)
    cos_tab, sin_tab = _rope_tables(pos)
    if seq < tm:
        cos_tab = jnp.tile(cos_tab, (tm // seq, 1))
        sin_tab = jnp.tile(sin_tab, (tm // seq, 1))
    h1, q, k, v, xb, gb = _ffn_in(x.reshape(n, D_MODEL), cos_tab, sin_tab, w)
    mask_initial_keys = k_past is None
    if mask_initial_keys:
        k0 = jnp.zeros((batch, WINDOW, KV_WIDTH), _F32)
        v0 = jnp.zeros((batch, WINDOW, KV_WIDTH), _F32)
    else:
        k0 = k_past.reshape(batch, WINDOW, KV_WIDTH)
        v0 = v_past.reshape(batch, WINDOW, KV_WIDTH)
    conv0 = jnp.pad(conv_prev, ((0, 0), (V7X_SUBLANES - (CONV_WIDTH - 1), 0), (0, 0)))
    hst0 = h_prev.reshape(batch, 1, LRU_WIDTH)
    h2, new_h = _mixer(h1, q, k, v, xb, gb, k0, v0, conv0, hst0, w,
                       batch=batch, seq=seq, mask_initial_keys=mask_initial_keys)
    y = _ffn_out(h2, p.reshape(n, PLE_DIM), w)
    k_all = jnp.concatenate([k0, k.reshape(batch, seq, KV_WIDTH)], axis=1)[:, -WINDOW:]
    v_all = jnp.concatenate([v0, v.reshape(batch, seq, KV_WIDTH)], axis=1)[:, -WINDOW:]
    conv_all = jnp.concatenate([conv_prev, xb.reshape(batch, seq, LRU_WIDTH)], axis=1)[:, -(CONV_WIDTH - 1):]
    return (y.reshape(batch, seq, D_MODEL),
            k_all.reshape(batch, WINDOW, N_KV_HEADS, HEAD_DIM),
            v_all.reshape(batch, WINDOW, N_KV_HEADS, HEAD_DIM),
            conv_all, new_h.reshape(batch, LRU_WIDTH))


def kernel(x_prompt, x_sample, p_prompt, p_sample, cache_k, cache_v, state_conv, state_h, ffn1_wg, ffn1_wu, ffn1_wd, ln1_g, ln1_b, w_in, attn_sinks, conv_w, conv_b, lru_wa, lru_ba, lru_wx, lru_bx, lru_lambda, w_out, ln2_g, ln2_b, ffn2_wg, ffn2_wu, ffn2_wd, ln3_g, ln3_b, w_ple, w_ple_gate):
    pos_prompt = jnp.arange(x_prompt.shape[1])
    pos_sample = PAST_LEN + jnp.arange(x_sample.shape[1])
    yp, ys = x_prompt, x_sample
    outs_p, outs_s = [], []
    for i in range(DEPTH):
        w = _prepare_weights(i, ffn1_wg, ffn1_wu, ffn1_wd, ln1_g, ln1_b, w_in, attn_sinks, conv_w, conv_b,
                             lru_wa, lru_ba, lru_wx, lru_bx, lru_lambda, w_out, ln2_g, ln2_b,
                             ffn2_wg, ffn2_wu, ffn2_wd, ln3_g, ln3_b, w_ple, w_ple_gate)
        bp = yp.shape[0]
        conv_zero = jnp.zeros((bp, CONV_WIDTH - 1, LRU_WIDTH), yp.dtype)
        h_zero = jnp.zeros((bp, LRU_WIDTH), _F32)
        yp, *rest_p = _layer(yp, p_prompt[i], pos_prompt, None, None, conv_zero, h_zero, w)
        ys, *rest_s = _layer(ys, p_sample[i], pos_sample, cache_k[i], cache_v[i], state_conv[i], state_h[i], w)
        outs_p.append(rest_p)
        outs_s.append(rest_s)
    stack = lambda outs, j: jnp.stack([o[j] for o in outs])
    return (yp, ys,
            stack(outs_p, 0), stack(outs_p, 1), stack(outs_p, 2), stack(outs_p, 3),
            stack(outs_s, 0), stack(outs_s, 1), stack(outs_s, 2), stack(outs_s, 3))
```

```python
import functools

import jax
import jax.numpy as jnp
from jax import lax
from jax.experimental import pallas as pl
from jax.experimental.pallas import tpu as pltpu

D_MODEL = 1024
CHUNK = 64
N_HEADS = 8
N_KV_HEADS = 2
GROUP = N_HEADS // N_KV_HEADS
HEAD_DIM = 64
ATTN_WIDTH = N_HEADS * HEAD_DIM
KV_WIDTH = N_KV_HEADS * HEAD_DIM
WINDOW = 128
ROT_DIM = HEAD_DIM // 4
ROPE_THETA = 500000.0
LRU_WIDTH = D_MODEL - ATTN_WIDTH
LRU_BLOCKS = 8
LRU_BLOCK = LRU_WIDTH // LRU_BLOCKS
CONV_WIDTH = 4
LRU_C = 8.0
IN_COLS = ATTN_WIDTH + 2 * KV_WIDTH + 2 * LRU_WIDTH
D_FF = 2816
PLE_DIM = 256
DEPTH = 1
PAST_LEN = 4096
ALPHA = (2.0 * DEPTH) ** 0.25
LN_EPS = 1e-5
NEG_INF = -1e30

V7X_LANES = 128
V7X_SUBLANES = 8
V7X_MXU_COLS = 256
V7X_VMEM_BYTES = 64 * 1024 * 1024
LRU_SLABS = LRU_WIDTH // V7X_LANES

_HEAD_SLOTS = tuple(h for j in range(GROUP) for h in (j, GROUP + j))

_BF16 = jnp.bfloat16
_F32 = jnp.float32


def _ff_chunks():
    step = 4 * V7X_MXU_COLS
    return tuple((lo, min(lo + step, D_FF)) for lo in range(0, D_FF, step))


def _layer_norm(y, g, b):
    mu = jnp.mean(y, -1, keepdims=True)
    var = jnp.mean(jnp.square(y - mu), -1, keepdims=True)
    return (y - mu) * lax.rsqrt(var + LN_EPS) * g + b


def _macaron(x, wg_ref, wu_ref, wd_ref, g, b):
    x16 = x.astype(_BF16)
    acc = None
    for lo, hi in _ff_chunks():
        gate = jnp.dot(x16, wg_ref[:, lo:hi], preferred_element_type=_F32)
        up = jnp.dot(x16, wu_ref[:, lo:hi], preferred_element_type=_F32)
        act = (jax.nn.silu(gate) * up).astype(_BF16)
        part = jnp.dot(act, wd_ref[lo:hi, :], preferred_element_type=_F32)
        acc = part if acc is None else acc + part
    return _layer_norm(ALPHA * x + 0.5 * acc, g, b)


def _rope_slab(xs, cos, sin_signed, low_half):
    up = pltpu.roll(xs, V7X_LANES - ROT_DIM // 2, axis=1)
    dn = pltpu.roll(xs, ROT_DIM // 2, axis=1)
    return xs * cos + jnp.where(low_half, up, dn) * sin_signed


def _ffn_in_kernel(x_ref, cos_ref, sin_ref, wg_ref, wu_ref, wd_ref, g_ref, b_ref, win_ref,
                   h_ref, q_ref, k_ref, v_ref, xb_ref, gb_ref):
    h = _macaron(x_ref[...], wg_ref, wu_ref, wd_ref, g_ref[...], b_ref[...])
    h_ref[...] = h
    z = jnp.dot(h.astype(_BF16), win_ref[...], preferred_element_type=_F32)
    cos = cos_ref[...]
    sin = sin_ref[...]
    lane = lax.broadcasted_iota(jnp.int32, cos.shape, 1)
    low_half = (lane & (HEAD_DIM - 1)) < (ROT_DIM // 2)
    for j in range(ATTN_WIDTH // V7X_LANES):
        sl = slice(j * V7X_LANES, (j + 1) * V7X_LANES)
        q_ref[:, sl] = _rope_slab(z[:, sl], cos, sin, low_half)
    o1 = ATTN_WIDTH
    o2 = o1 + KV_WIDTH
    o3 = o2 + KV_WIDTH
    o4 = o3 + LRU_WIDTH
    k_ref[...] = _rope_slab(z[:, o1:o2], cos, sin, low_half)
    v_ref[...] = z[:, o2:o3]
    xb_ref[...] = z[:, o3:o4]
    gb_ref[...] = z[:, o4:]


def _attention_chunk(q16, kc, vc, sinks_ref, masked_keys):
    lane = lax.broadcasted_iota(jnp.int32, (CHUNK, V7X_LANES), 1)
    outs = []
    for vh in range(N_KV_HEADS):
        own = (lane >= HEAD_DIM) if vh else (lane < HEAD_DIM)
        qs = [jnp.where(own, q16[:, j * V7X_LANES:(j + 1) * V7X_LANES], jnp.zeros((), _BF16))
              for j in range(GROUP)]
        q4 = jnp.concatenate(qs, axis=0)
        s = lax.dot_general(q4, kc, (((1,), (1,)), ((), ())),
                            preferred_element_type=_F32) * (HEAD_DIM ** -0.5)
        if masked_keys is not None:
            s = jnp.where(masked_keys, NEG_INF, s)
        ws = []
        for g in range(GROUP):
            sg = s[g * CHUNK:(g + 1) * CHUNK]
            sink = sinks_ref[vh * GROUP + g]
            m = jnp.maximum(jnp.max(sg, axis=-1, keepdims=True), sink)
            pr = jnp.exp(sg - m)
            denom = jnp.sum(pr, axis=-1, keepdims=True) + jnp.exp(sink - m)
            ws.append((pr * (1.0 / denom)).astype(_BF16))
        w = jnp.concatenate(ws, axis=0)
        outs.append(jnp.dot(w, vc, preferred_element_type=_F32))
    low = lane < HEAD_DIM
    slabs = [jnp.where(low, outs[0][g * CHUNK:(g + 1) * CHUNK], outs[1][g * CHUNK:(g + 1) * CHUNK])
             for g in range(GROUP)]
    return jnp.concatenate(slabs, axis=1)


def _lru_scan(a_buf, u_buf, loc_buf, cum_buf, cin_buf, hs_buf, h0, rows):
    groups = rows // V7X_SUBLANES
    lasts = []
    for s in range(LRU_SLABS):
        def slab(ref, r):
            return ref.at[s, pl.ds(r, groups, stride=V7X_SUBLANES), :]

        loc = slab(u_buf, 0)[...]
        cum = slab(a_buf, 0)[...]
        slab(loc_buf, 0)[...] = loc
        slab(cum_buf, 0)[...] = cum
        for r in range(1, V7X_SUBLANES):
            ar = slab(a_buf, r)[...]
            loc = ar * loc + slab(u_buf, r)[...]
            cum = ar * cum
            slab(loc_buf, r)[...] = loc
            slab(cum_buf, r)[...] = cum
        carry = h0[:, s * V7X_LANES:(s + 1) * V7X_LANES]
        for gidx in range(groups):
            cin_buf[s, gidx:gidx + 1, :] = carry
            carry = cum[gidx:gidx + 1, :] * carry + loc[gidx:gidx + 1, :]
        cin = cin_buf[s, 0:groups, :]
        for r in range(V7X_SUBLANES):
            slab(hs_buf, r)[...] = slab(cum_buf, r)[...] * cin + slab(loc_buf, r)[...]
        lasts.append(carry)
    return jnp.concatenate(lasts, axis=1)


def _mixer_kernel(sinks_ref, h1_ref, q_ref, k_ref, v_ref, xb_ref, gb_ref,
                  k0_ref, v0_ref, conv0_ref, hst0_ref,
                  convw_ref, convb_ref, wax_ref, ba_ref, bx_ref, lam_ref,
                  wout_ref, g_ref, b_ref,
                  h2_ref, hst_ref,
                  kbuf, vbuf, xbuf, a_buf, u_buf, loc_buf, cum_buf, cin_buf, hs_buf, hcar,
                  *, rows, mask_initial_keys):
    t = pl.program_id(1)
    conv_pad = V7X_SUBLANES

    @pl.when(t == 0)
    def _():
        kbuf[0:WINDOW, :] = k0_ref[...].astype(_BF16)
        vbuf[0:WINDOW, :] = v0_ref[...].astype(_BF16)
        xbuf[0:conv_pad, :] = conv0_ref[...]
        hcar[...] = hst0_ref[...]

    kbuf[WINDOW:WINDOW + rows, :] = k_ref[...].astype(_BF16)
    vbuf[WINDOW:WINDOW + rows, :] = v_ref[...].astype(_BF16)
    xb = xb_ref[...]
    xbuf[conv_pad:conv_pad + rows, :] = xb

    q16 = q_ref[...].astype(_BF16)
    key_idx = lax.broadcasted_iota(jnp.int32, (GROUP * CHUNK, WINDOW + CHUNK), 1)
    attn_chunks = []
    for c in range(rows // CHUNK):
        kc = kbuf[c * CHUNK:c * CHUNK + WINDOW + CHUNK, :]
        vc = vbuf[c * CHUNK:c * CHUNK + WINDOW + CHUNK, :]
        masked = None
        if mask_initial_keys and c * CHUNK < WINDOW:
            masked = jnp.logical_and(t == 0, key_idx < WINDOW - c * CHUNK)
        attn_chunks.append(_attention_chunk(q16[c * CHUNK:(c + 1) * CHUNK, :], kc, vc, sinks_ref, masked))
    attn = jnp.concatenate(attn_chunks, axis=0)

    cw = convw_ref[...]
    xc = convb_ref[...] + cw[0:1, :] * xbuf[conv_pad - 3:conv_pad - 3 + rows, :]
    for j in range(1, CONV_WIDTH):
        xc = xc + cw[j:j + 1, :] * xbuf[conv_pad - 3 + j:conv_pad - 3 + j + rows, :]
    gates = jnp.dot(xc.astype(_BF16), wax_ref[...], preferred_element_type=_F32)
    r_gate = jax.nn.sigmoid(gates[:, :LRU_WIDTH] + ba_ref[...])
    i_gate = jax.nn.sigmoid(gates[:, LRU_WIDTH:] + bx_ref[...])
    log_a = (-LRU_C * jax.nn.softplus(-lam_ref[...])) * r_gate
    a_gate = jnp.exp(log_a)
    for s in range(LRU_SLABS):
        a_buf[s] = a_gate[:, s * V7X_LANES:(s + 1) * V7X_LANES]
    one_minus_a2 = -jnp.tanh(log_a) * (a_gate * a_gate + 1.0)
    u_val = jnp.sqrt(one_minus_a2) * (i_gate * xc)
    for s in range(LRU_SLABS):
        u_buf[s] = u_val[:, s * V7X_LANES:(s + 1) * V7X_LANES]
    last = _lru_scan(a_buf, u_buf, loc_buf, cum_buf, cin_buf, hs_buf, hcar[...], rows)
    hcar[...] = last
    hst_ref[...] = last
    hs = jnp.concatenate([hs_buf[s] for s in range(LRU_SLABS)], axis=1)
    lru = hs * jax.nn.gelu(gb_ref[...])

    mix = jnp.dot(jnp.concatenate([attn, lru], axis=1).astype(_BF16), wout_ref[...],
                  preferred_element_type=_F32)
    h2_ref[...] = _layer_norm(ALPHA * h1_ref[...] + mix, g_ref[...], b_ref[...])

    kbuf[0:WINDOW, :] = kbuf[rows:rows + WINDOW, :]
    vbuf[0:WINDOW, :] = vbuf[rows:rows + WINDOW, :]
    xbuf[0:conv_pad, :] = xbuf[rows:rows + conv_pad, :]


def _ffn_out_kernel(h_ref, p_ref, wg_ref, wu_ref, wd_ref, g_ref, b_ref, wgate_ref, wple_ref, y_ref):
    h = _macaron(h_ref[...], wg_ref, wu_ref, wd_ref, g_ref[...], b_ref[...])
    gate = jax.nn.sigmoid(jnp.dot(h.astype(_BF16), wgate_ref[...], preferred_element_type=_F32))
    ple = jnp.dot(p_ref[...].astype(_BF16), wple_ref[...], preferred_element_type=_F32)
    y_ref[...] = h + gate * ple


def _resident(shape):
    zeros = (0,) * len(shape)
    return pl.BlockSpec(shape, lambda *_: zeros, pipeline_mode=pl.Buffered(1))


def _row_tile(total_rows):
    return min(512, total_rows)


def _vmem_limit():
    return V7X_VMEM_BYTES - 6 * 1024 * 1024


def _ffn_in(x2d, cos_tab, sin_tab, w):
    n = x2d.shape[0]
    tm = _row_tile(n)
    tab_tiles = cos_tab.shape[0] // tm
    row = lambda width: pl.BlockSpec((tm, width), lambda i: (i, 0))
    tab = pl.BlockSpec((tm, V7X_LANES), lambda i: (i % tab_tiles, 0))
    widths = (D_MODEL, ATTN_WIDTH, KV_WIDTH, KV_WIDTH, LRU_WIDTH, LRU_WIDTH)
    return pl.pallas_call(
        _ffn_in_kernel,
        grid=(n // tm,),
        in_specs=[row(D_MODEL), tab, tab,
                  _resident((D_MODEL, D_FF)), _resident((D_MODEL, D_FF)), _resident((D_FF, D_MODEL)),
                  _resident((1, D_MODEL)), _resident((1, D_MODEL)), _resident((D_MODEL, IN_COLS))],
        out_specs=[row(wd) for wd in widths],
        out_shape=[jax.ShapeDtypeStruct((n, wd), _F32) for wd in widths],
        compiler_params=pltpu.CompilerParams(dimension_semantics=("arbitrary",),
                                             vmem_limit_bytes=_vmem_limit()),
        name="ffn_in",
    )(x2d, cos_tab, sin_tab, w["ffn1_wg"], w["ffn1_wu"], w["ffn1_wd"], w["ln1_g"], w["ln1_b"], w["w_in"])


def _mixer(h1, q, k, v, xb, gb, k0, v0, conv0, hst0, w, *, batch, seq, mask_initial_keys):
    tm = _row_tile(seq)
    nt = seq // tm
    row = lambda width: pl.BlockSpec((tm, width), lambda b, t: (b * nt + t, 0))
    per_batch = lambda r, c: pl.BlockSpec((None, r, c), lambda b, t: (b, 0, 0))
    lru_buf = pltpu.VMEM((LRU_SLABS, tm, V7X_LANES), _F32)
    kernel = functools.partial(_mixer_kernel, rows=tm, mask_initial_keys=mask_initial_keys)
    return pl.pallas_call(
        kernel,
        grid=(batch, nt),
        in_specs=[pl.BlockSpec(memory_space=pltpu.SMEM),
                  row(D_MODEL), row(ATTN_WIDTH), row(KV_WIDTH), row(KV_WIDTH), row(LRU_WIDTH), row(LRU_WIDTH),
                  per_batch(WINDOW, KV_WIDTH), per_batch(WINDOW, KV_WIDTH),
                  per_batch(V7X_SUBLANES, LRU_WIDTH), per_batch(1, LRU_WIDTH),
                  _resident((CONV_WIDTH, LRU_WIDTH)), _resident((1, LRU_WIDTH)),
                  _resident((LRU_WIDTH, 2 * LRU_WIDTH)), _resident((1, LRU_WIDTH)), _resident((1, LRU_WIDTH)),
                  _resident((1, LRU_WIDTH)),
                  _resident((D_MODEL, D_MODEL)), _resident((1, D_MODEL)), _resident((1, D_MODEL))],
        out_specs=[row(D_MODEL), per_batch(1, LRU_WIDTH)],
        out_shape=[jax.ShapeDtypeStruct((batch * seq, D_MODEL), _F32),
                   jax.ShapeDtypeStruct((batch, 1, LRU_WIDTH), _F32)],
        scratch_shapes=[pltpu.VMEM((WINDOW + tm, KV_WIDTH), _BF16), pltpu.VMEM((WINDOW + tm, KV_WIDTH), _BF16),
                        pltpu.VMEM((V7X_SUBLANES + tm, LRU_WIDTH), _F32),
                        lru_buf, lru_buf, lru_buf, lru_buf,
                        pltpu.VMEM((LRU_SLABS, max(tm // V7X_SUBLANES, V7X_SUBLANES), V7X_LANES), _F32),
                        lru_buf, pltpu.VMEM((1, LRU_WIDTH), _F32)],
        compiler_params=pltpu.CompilerParams(dimension_semantics=("arbitrary", "arbitrary"),
                                             vmem_limit_bytes=_vmem_limit()),
        name="mixer",
    )(w["attn_sinks"], h1, q, k, v, xb, gb, k0, v0, conv0, hst0,
      w["conv_w"], w["conv_b"], w["lru_wax"], w["lru_ba"], w["lru_bx"], w["lru_lambda"],
      w["w_out"], w["ln2_g"], w["ln2_b"])


def _ffn_out(h2d, p2d, w):
    n = h2d.shape[0]
    tm = _row_tile(n)
    row = lambda width: pl.BlockSpec((tm, width), lambda i: (i, 0))
    return pl.pallas_call(
        _ffn_out_kernel,
        grid=(n // tm,),
        in_specs=[row(D_MODEL), row(PLE_DIM),
                  _resident((D_MODEL, D_FF)), _resident((D_MODEL, D_FF)), _resident((D_FF, D_MODEL)),
                  _resident((1, D_MODEL)), _resident((1, D_MODEL)),
                  _resident((D_MODEL, D_MODEL)), _resident((PLE_DIM, D_MODEL))],
        out_specs=row(D_MODEL),
        out_shape=jax.ShapeDtypeStruct((n, D_MODEL), _F32),
        compiler_params=pltpu.CompilerParams(dimension_semantics=("arbitrary",),
                                             vmem_limit_bytes=_vmem_limit()),
        name="ffn_out",
    )(h2d, p2d, w["ffn2_wg"], w["ffn2_wu"], w["ffn2_wd"], w["ln3_g"], w["ln3_b"], w["w_ple_gate"], w["w_ple"])


def _rope_tables(pos):
    half = ROT_DIM // 2
    inv = jnp.power(jnp.float32(ROPE_THETA), -jnp.arange(half, dtype=_F32) * (2.0 / ROT_DIM))
    ang = pos.astype(_F32)[:, None] * inv[None, :]
    cos, sin = jnp.cos(ang), jnp.sin(ang)
    n = pos.shape[0]
    ident = jnp.ones((n, HEAD_DIM - ROT_DIM), _F32)
    cos_head = jnp.concatenate([cos, cos, ident], axis=1)
    sin_head = jnp.concatenate([-sin, sin, 0.0 * ident], axis=1)
    reps = V7X_LANES // HEAD_DIM
    return jnp.tile(cos_head, (1, reps)), jnp.tile(sin_head, (1, reps))


def _prepare_weights(i, ffn1_wg, ffn1_wu, ffn1_wd, ln1_g, ln1_b, w_in, attn_sinks, conv_w, conv_b,
                     lru_wa, lru_ba, lru_wx, lru_bx, lru_lambda, w_out, ln2_g, ln2_b,
                     ffn2_wg, ffn2_wu, ffn2_wd, ln3_g, ln3_b, w_ple, w_ple_gate):
    row = lambda a: a[i].reshape(1, -1)
    slots = jnp.asarray(_HEAD_SLOTS)
    win = w_in[i]
    wq = win[:, :ATTN_WIDTH].reshape(D_MODEL, N_HEADS, HEAD_DIM)[:, slots].reshape(D_MODEL, ATTN_WIDTH)
    wout = w_out[i]
    wo_attn = wout[:ATTN_WIDTH].reshape(N_HEADS, HEAD_DIM, D_MODEL)[slots].reshape(ATTN_WIDTH, D_MODEL)
    eye = jnp.eye(LRU_BLOCKS, dtype=_F32)
    blockdiag = lambda wb: jnp.einsum("ncd,nm->ncmd", wb, eye).reshape(LRU_WIDTH, LRU_WIDTH)
    return dict(
        ffn1_wg=ffn1_wg[i].astype(_BF16), ffn1_wu=ffn1_wu[i].astype(_BF16), ffn1_wd=ffn1_wd[i].astype(_BF16),
        ln1_g=row(ln1_g), ln1_b=row(ln1_b),
        w_in=jnp.concatenate([wq, win[:, ATTN_WIDTH:]], axis=1).astype(_BF16),
        attn_sinks=attn_sinks[i],
        conv_w=conv_w[i], conv_b=row(conv_b),
        lru_wax=jnp.concatenate([blockdiag(lru_wa[i]), blockdiag(lru_wx[i])], axis=1).astype(_BF16),
        lru_ba=row(lru_ba), lru_bx=row(lru_bx), lru_lambda=row(lru_lambda),
        w_out=jnp.concatenate([wo_attn, wout[ATTN_WIDTH:]], axis=0).astype(_BF16),
        ln2_g=row(ln2_g), ln2_b=row(ln2_b),
        ffn2_wg=ffn2_wg[i].astype(_BF16), ffn2_wu=ffn2_wu[i].astype(_BF16), ffn2_wd=ffn2_wd[i].astype(_BF16),
        ln3_g=row(ln3_g), ln3_b=row(ln3_b),
        w_ple=w_ple[i].astype(_BF16), w_ple_gate=w_ple_gate[i].astype(_BF16))


def _layer(x, p, pos, k_past, v_past, conv_prev, h_prev, w):
    batch, seq = x.shape[0], x.shape[1]
    n = batch * seq
    tm = _row_tile(n)
    cos_tab, sin_tab = _rope_tables(pos)
    if seq < tm:
        cos_tab = jnp.tile(cos_tab, (tm // seq, 1))
        sin_tab = jnp.tile(sin_tab, (tm // seq, 1))
    h1, q, k, v, xb, gb = _ffn_in(x.reshape(n, D_MODEL), cos_tab, sin_tab, w)
    mask_initial_keys = k_past is None
    if mask_initial_keys:
        k0 = jnp.zeros((batch, WINDOW, KV_WIDTH), _F32)
        v0 = jnp.zeros((batch, WINDOW, KV_WIDTH), _F32)
    else:
        k0 = k_past.reshape(batch, WINDOW, KV_WIDTH)
        v0 = v_past.reshape(batch, WINDOW, KV_WIDTH)
    conv0 = jnp.pad(conv_prev, ((0, 0), (V7X_SUBLANES - (CONV_WIDTH - 1), 0), (0, 0)))
    hst0 = h_prev.reshape(batch, 1, LRU_WIDTH)
    h2, new_h = _mixer(h1, q, k, v, xb, gb, k0, v0, conv0, hst0, w,
                       batch=batch, seq=seq, mask_initial_keys=mask_initial_keys)
    y = _ffn_out(h2, p.reshape(n, PLE_DIM), w)
    k_all = jnp.concatenate([k0, k.reshape(batch, seq, KV_WIDTH)], axis=1)[:, -WINDOW:]
    v_all = jnp.concatenate([v0, v.reshape(batch, seq, KV_WIDTH)], axis=1)[:, -WINDOW:]
    conv_all = jnp.concatenate([conv_prev, xb.reshape(batch, seq, LRU_WIDTH)], axis=1)[:, -(CONV_WIDTH - 1):]
    return (y.reshape(batch, seq, D_MODEL),
            k_all.reshape(batch, WINDOW, N_KV_HEADS, HEAD_DIM),
            v_all.reshape(batch, WINDOW, N_KV_HEADS, HEAD_DIM),
            conv_all, new_h.reshape(batch, LRU_WIDTH))


def kernel(x_prompt, x_sample, p_prompt, p_sample, cache_k, cache_v, state_conv, state_h, ffn1_wg, ffn1_wu, ffn1_wd, ln1_g, ln1_b, w_in, attn_sinks, conv_w, conv_b, lru_wa, lru_ba, lru_wx, lru_bx, lru_lambda, w_out, ln2_g, ln2_b, ffn2_wg, ffn2_wu, ffn2_wd, ln3_g, ln3_b, w_ple, w_ple_gate):
    pos_prompt = jnp.arange(x_prompt.shape[1])
    pos_sample = PAST_LEN + jnp.arange(x_sample.shape[1])
    yp, ys = x_prompt, x_sample
    outs_p, outs_s = [], []
    for i in range(DEPTH):
        w = _prepare_weights(i, ffn1_wg, ffn1_wu, ffn1_wd, ln1_g, ln1_b, w_in, attn_sinks, conv_w, conv_b,
                             lru_wa, lru_ba, lru_wx, lru_bx, lru_lambda, w_out, ln2_g, ln2_b,
                             ffn2_wg, ffn2_wu, ffn2_wd, ln3_g, ln3_b, w_ple, w_ple_gate)
        bp = yp.shape[0]
        conv_zero = jnp.zeros((bp, CONV_WIDTH - 1, LRU_WIDTH), yp.dtype)
        h_zero = jnp.zeros((bp, LRU_WIDTH), _F32)
        yp, *rest_p = _layer(yp, p_prompt[i], pos_prompt, None, None, conv_zero, h_zero, w)
        ys, *rest_s = _layer(ys, p_sample[i], pos_sample, cache_k[i], cache_v[i], state_conv[i], state_h[i], w)
        outs_p.append(rest_p)
        outs_s.append(rest_s)
    stack = lambda outs, j: jnp.stack([o[j] for o in outs])
    return (yp, ys,
            stack(outs_p, 0), stack(outs_p, 1), stack(outs_p, 2), stack(outs_p, 3),
            stack(outs_s, 0), stack(outs_s, 1), stack(outs_s, 2), stack(outs_s, 3))
```

```python
import functools

import jax
import jax.numpy as jnp
from jax import lax
from jax.experimental import pallas as pl
from jax.experimental.pallas import tpu as pltpu

D_MODEL = 1024
CHUNK = 64
N_HEADS = 8
N_KV_HEADS = 2
GROUP = N_HEADS // N_KV_HEADS
HEAD_DIM = 64
ATTN_WIDTH = N_HEADS * HEAD_DIM
KV_WIDTH = N_KV_HEADS * HEAD_DIM
WINDOW = 128
ROT_DIM = HEAD_DIM // 4
ROPE_THETA = 500000.0
LRU_WIDTH = D_MODEL - ATTN_WIDTH
LRU_BLOCKS = 8
LRU_BLOCK = LRU_WIDTH // LRU_BLOCKS
CONV_WIDTH = 4
LRU_C = 8.0
IN_COLS = ATTN_WIDTH + 2 * KV_WIDTH + 2 * LRU_WIDTH
D_FF = 2816
PLE_DIM = 256
DEPTH = 1
PAST_LEN = 4096
ALPHA = (2.0 * DEPTH) ** 0.25
LN_EPS = 1e-5
NEG_INF = -1e30

V7X_LANES = 128
V7X_SUBLANES = 8
V7X_VMEM_BYTES = 64 * 1024 * 1024
LRU_SLABS = LRU_WIDTH // V7X_LANES

_HEAD_SLOTS = tuple(h for j in range(GROUP) for h in (j, GROUP + j))

_BF16 = jnp.bfloat16
_F32 = jnp.float32


def _row_parts(rows):
    half = rows // 2
    return (slice(0, half), slice(half, rows))


def _layer_norm(y, g, b):
    mu = jnp.mean(y, -1, keepdims=True)
    var = jnp.mean(jnp.square(y - mu), -1, keepdims=True)
    return (y - mu) * lax.rsqrt(var + LN_EPS) * g + b


def _swiglu(x, wg_ref, wu_ref, wd_ref):
    x16 = x.astype(_BF16)
    gate = jnp.dot(x16, wg_ref[...], preferred_element_type=_F32)
    up = jnp.dot(x16, wu_ref[...], preferred_element_type=_F32)
    act = (jax.nn.silu(gate) * up).astype(_BF16)
    return jnp.dot(act, wd_ref[...], preferred_element_type=_F32)


def _rope_slab(xs, cos, sin_signed, low_half):
    up = pltpu.roll(xs, V7X_LANES - ROT_DIM // 2, axis=1)
    dn = pltpu.roll(xs, ROT_DIM // 2, axis=1)
    return xs * cos + jnp.where(low_half, up, dn) * sin_signed


def _ffn_in_kernel(x_ref, cos_ref, sin_ref, wg_ref, wu_ref, wd_ref, g_ref, b_ref, win_ref,
                   h_ref, q_ref, k_ref, v_ref, xb_ref, gb_ref):
    parts = _row_parts(x_ref.shape[0])
    xs = [x_ref[p, :] for p in parts]
    ffs = [_swiglu(x, wg_ref, wu_ref, wd_ref) for x in xs]
    o1 = ATTN_WIDTH
    o2 = o1 + KV_WIDTH
    o3 = o2 + KV_WIDTH
    o4 = o3 + LRU_WIDTH
    for p, x, ff in zip(parts, xs, ffs):
        h = _layer_norm(ALPHA * x + 0.5 * ff, g_ref[...], b_ref[...])
        h_ref[p, :] = h
        z = jnp.dot(h.astype(_BF16), win_ref[...], preferred_element_type=_F32)
        cos = cos_ref[p, :]
        sin = sin_ref[p, :]
        lane = lax.broadcasted_iota(jnp.int32, cos.shape, 1)
        low_half = (lane & (HEAD_DIM - 1)) < (ROT_DIM // 2)
        for j in range(ATTN_WIDTH // V7X_LANES):
            sl = slice(j * V7X_LANES, (j + 1) * V7X_LANES)
            q_ref[p, sl] = _rope_slab(z[:, sl], cos, sin, low_half)
        k_ref[p, :] = _rope_slab(z[:, o1:o2], cos, sin, low_half)
        v_ref[p, :] = z[:, o2:o3]
        xb_ref[p, :] = z[:, o3:o4]
        gb_ref[p, :] = z[:, o4:]


def _attention_chunk(q16, kc, vc, sinks_ref, masked_keys):
    lane = lax.broadcasted_iota(jnp.int32, (CHUNK, V7X_LANES), 1)
    outs = []
    for vh in range(N_KV_HEADS):
        own = (lane >= HEAD_DIM) if vh else (lane < HEAD_DIM)
        qs = [jnp.where(own, q16[:, j * V7X_LANES:(j + 1) * V7X_LANES], jnp.zeros((), _BF16))
              for j in range(GROUP)]
        q4 = jnp.concatenate(qs, axis=0)
        s = lax.dot_general(q4, kc, (((1,), (1,)), ((), ())),
                            preferred_element_type=_F32) * (HEAD_DIM ** -0.5)
        if masked_keys is not None:
            s = jnp.where(masked_keys, NEG_INF, s)
        ws = []
        for g in range(GROUP):
            sg = s[g * CHUNK:(g + 1) * CHUNK]
            sink = sinks_ref[vh * GROUP + g]
            m = jnp.maximum(jnp.max(sg, axis=-1, keepdims=True), sink)
            pr = jnp.exp(sg - m)
            denom = jnp.sum(pr, axis=-1, keepdims=True) + jnp.exp(sink - m)
            ws.append((pr * (1.0 / denom)).astype(_BF16))
        w = jnp.concatenate(ws, axis=0)
        outs.append(jnp.dot(w, vc, preferred_element_type=_F32))
    low = lane < HEAD_DIM
    slabs = [jnp.where(low, outs[0][g * CHUNK:(g + 1) * CHUNK], outs[1][g * CHUNK:(g + 1) * CHUNK])
             for g in range(GROUP)]
    return jnp.concatenate(slabs, axis=1)


def _lru_scan(a_buf, u_buf, loc_buf, cum_buf, cin_buf, hs_buf, h0, rows):
    groups = rows // V7X_SUBLANES
    lasts = []
    for s in range(LRU_SLABS):
        def slab(ref, r):
            return ref.at[s, pl.ds(r, groups, stride=V7X_SUBLANES), :]

        loc = slab(u_buf, 0)[...]
        cum = slab(a_buf, 0)[...]
        slab(loc_buf, 0)[...] = loc
        slab(cum_buf, 0)[...] = cum
        for r in range(1, V7X_SUBLANES):
            ar = slab(a_buf, r)[...]
            loc = ar * loc + slab(u_buf, r)[...]
            cum = ar * cum
            slab(loc_buf, r)[...] = loc
            slab(cum_buf, r)[...] = cum
        carry = h0[:, s * V7X_LANES:(s + 1) * V7X_LANES]
        for gidx in range(groups):
            cin_buf[s, gidx:gidx + 1, :] = carry
            carry = cum[gidx:gidx + 1, :] * carry + loc[gidx:gidx + 1, :]
        cin = cin_buf[s, 0:groups, :]
        for r in range(V7X_SUBLANES):
            slab(hs_buf, r)[...] = slab(cum_buf, r)[...] * cin + slab(loc_buf, r)[...]
        lasts.append(carry)
    return jnp.concatenate(lasts, axis=1)


def _mixer_kernel(sinks_ref, h1_ref, q_ref, k_ref, v_ref, xb_ref, gb_ref,
                  k0_ref, v0_ref, conv0_ref, hst0_ref,
                  convw_ref, convb_ref, wax_ref, ba_ref, bx_ref, lam_ref,
                  wout_ref, g_ref, b_ref,
                  h2_ref, hst_ref,
                  kbuf, vbuf, xbuf, a_buf, u_buf, loc_buf, cum_buf, cin_buf, hs_buf, hcar,
                  *, rows, mask_initial_keys):
    t = pl.program_id(1)
    conv_pad = V7X_SUBLANES

    @pl.when(t == 0)
    def _():
        kbuf[0:WINDOW, :] = k0_ref[...].astype(_BF16)
        vbuf[0:WINDOW, :] = v0_ref[...].astype(_BF16)
        xbuf[0:conv_pad, :] = conv0_ref[...]
        hcar[...] = hst0_ref[...]

    kbuf[WINDOW:WINDOW + rows, :] = k_ref[...].astype(_BF16)
    vbuf[WINDOW:WINDOW + rows, :] = v_ref[...].astype(_BF16)
    xb = xb_ref[...]
    xbuf[conv_pad:conv_pad + rows, :] = xb

    q16 = q_ref[...].astype(_BF16)
    key_idx = lax.broadcasted_iota(jnp.int32, (GROUP * CHUNK, WINDOW + CHUNK), 1)
    attn_chunks = []
    for c in range(rows // CHUNK):
        kc = kbuf[c * CHUNK:c * CHUNK + WINDOW + CHUNK, :]
        vc = vbuf[c * CHUNK:c * CHUNK + WINDOW + CHUNK, :]
        masked = None
        if mask_initial_keys and c * CHUNK < WINDOW:
            masked = jnp.logical_and(t == 0, key_idx < WINDOW - c * CHUNK)
        attn_chunks.append(_attention_chunk(q16[c * CHUNK:(c + 1) * CHUNK, :], kc, vc, sinks_ref, masked))
    attn = jnp.concatenate(attn_chunks, axis=0)

    cw = convw_ref[...]
    xc = convb_ref[...] + cw[0:1, :] * xbuf[conv_pad - 3:conv_pad - 3 + rows, :]
    for j in range(1, CONV_WIDTH):
        xc = xc + cw[j:j + 1, :] * xbuf[conv_pad - 3 + j:conv_pad - 3 + j + rows, :]
    gates = jnp.dot(xc.astype(_BF16), wax_ref[...], preferred_element_type=_F32)
    r_gate = jax.nn.sigmoid(gates[:, :LRU_WIDTH] + ba_ref[...])
    i_gate = jax.nn.sigmoid(gates[:, LRU_WIDTH:] + bx_ref[...])
    log_a = (-LRU_C * jax.nn.softplus(-lam_ref[...])) * r_gate
    a_gate = jnp.exp(log_a)
    for s in range(LRU_SLABS):
        a_buf[s] = a_gate[:, s * V7X_LANES:(s + 1) * V7X_LANES]
    one_minus_a2 = -jnp.tanh(log_a) * (a_gate * a_gate + 1.0)
    u_val = jnp.sqrt(one_minus_a2) * (i_gate * xc)
    for s in range(LRU_SLABS):
        u_buf[s] = u_val[:, s * V7X_LANES:(s + 1) * V7X_LANES]
    last = _lru_scan(a_buf, u_buf, loc_buf, cum_buf, cin_buf, hs_buf, hcar[...], rows)
    hcar[...] = last
    hst_ref[...] = last
    hs = jnp.concatenate([hs_buf[s] for s in range(LRU_SLABS)], axis=1)
    lru = hs * jax.nn.gelu(gb_ref[...])

    mix = jnp.dot(jnp.concatenate([attn, lru], axis=1).astype(_BF16), wout_ref[...],
                  preferred_element_type=_F32)
    h2_ref[...] = _layer_norm(ALPHA * h1_ref[...] + mix, g_ref[...], b_ref[...])

    kbuf[0:WINDOW, :] = kbuf[rows:rows + WINDOW, :]
    vbuf[0:WINDOW, :] = vbuf[rows:rows + WINDOW, :]
    xbuf[0:conv_pad, :] = xbuf[rows:rows + conv_pad, :]


def _ffn_out_kernel(h_ref, p_ref, wg_ref, wu_ref, wd_ref, g_ref, b_ref, wgate_ref, wple_ref, y_ref):
    parts = _row_parts(h_ref.shape[0])
    xs = [h_ref[p, :] for p in parts]
    ffs = [_swiglu(x, wg_ref, wu_ref, wd_ref) for x in xs]
    for p, x, ff in zip(parts, xs, ffs):
        h = _layer_norm(ALPHA * x + 0.5 * ff, g_ref[...], b_ref[...])
        gate = jax.nn.sigmoid(jnp.dot(h.astype(_BF16), wgate_ref[...], preferred_element_type=_F32))
        ple = jnp.dot(p_ref[p, :].astype(_BF16), wple_ref[...], preferred_element_type=_F32)
        y_ref[p, :] = h + gate * ple


def _resident(shape):
    zeros = (0,) * len(shape)
    return pl.BlockSpec(shape, lambda *_: zeros, pipeline_mode=pl.Buffered(1))


def _row_tile(total_rows):
    return min(512, total_rows)


def _vmem_limit():
    return V7X_VMEM_BYTES - 6 * 1024 * 1024


def _ffn_in(x2d, cos_tab, sin_tab, w):
    n = x2d.shape[0]
    tm = _row_tile(n)
    tab_tiles = cos_tab.shape[0] // tm
    row = lambda width: pl.BlockSpec((tm, width), lambda i: (i, 0))
    tab = pl.BlockSpec((tm, V7X_LANES), lambda i: (i % tab_tiles, 0))
    widths = (D_MODEL, ATTN_WIDTH, KV_WIDTH, KV_WIDTH, LRU_WIDTH, LRU_WIDTH)
    return pl.pallas_call(
        _ffn_in_kernel,
        grid=(n // tm,),
        in_specs=[row(D_MODEL), tab, tab,
                  _resident((D_MODEL, D_FF)), _resident((D_MODEL, D_FF)), _resident((D_FF, D_MODEL)),
                  _resident((1, D_MODEL)), _resident((1, D_MODEL)), _resident((D_MODEL, IN_COLS))],
        out_specs=[row(wd) for wd in widths],
        out_shape=[jax.ShapeDtypeStruct((n, wd), _F32) for wd in widths],
        compiler_params=pltpu.CompilerParams(dimension_semantics=("arbitrary",),
                                             vmem_limit_bytes=_vmem_limit()),
        name="ffn_in",
    )(x2d, cos_tab, sin_tab, w["ffn1_wg"], w["ffn1_wu"], w["ffn1_wd"], w["ln1_g"], w["ln1_b"], w["w_in"])


def _mixer(h1, q, k, v, xb, gb, k0, v0, conv0, hst0, w, *, batch, seq, mask_initial_keys):
    tm = _row_tile(seq)
    nt = seq // tm
    row = lambda width: pl.BlockSpec((tm, width), lambda b, t: (b * nt + t, 0))
    per_batch = lambda r, c: pl.BlockSpec((None, r, c), lambda b, t: (b, 0, 0))
    lru_buf = pltpu.VMEM((LRU_SLABS, tm, V7X_LANES), _F32)
    kernel = functools.partial(_mixer_kernel, rows=tm, mask_initial_keys=mask_initial_keys)
    return pl.pallas_call(
        kernel,
        grid=(batch, nt),
        in_specs=[pl.BlockSpec(memory_space=pltpu.SMEM),
                  row(D_MODEL), row(ATTN_WIDTH), row(KV_WIDTH), row(KV_WIDTH), row(LRU_WIDTH), row(LRU_WIDTH),
                  per_batch(WINDOW, KV_WIDTH), per_batch(WINDOW, KV_WIDTH),
                  per_batch(V7X_SUBLANES, LRU_WIDTH), per_batch(1, LRU_WIDTH),
                  _resident((CONV_WIDTH, LRU_WIDTH)), _resident((1, LRU_WIDTH)),
                  _resident((LRU_WIDTH, 2 * LRU_WIDTH)), _resident((1, LRU_WIDTH)), _resident((1, LRU_WIDTH)),
                  _resident((1, LRU_WIDTH)),
                  _resident((D_MODEL, D_MODEL)), _resident((1, D_MODEL)), _resident((1, D_MODEL))],
        out_specs=[row(D_MODEL), per_batch(1, LRU_WIDTH)],
        out_shape=[jax.ShapeDtypeStruct((batch * seq, D_MODEL), _F32),
                   jax.ShapeDtypeStruct((batch, 1, LRU_WIDTH), _F32)],
        scratch_shapes=[pltpu.VMEM((WINDOW + tm, KV_WIDTH), _BF16), pltpu.VMEM((WINDOW + tm, KV_WIDTH), _BF16),
                        pltpu.VMEM((V7X_SUBLANES + tm, LRU_WIDTH), _F32),
                        lru_buf, lru_buf, lru_buf, lru_buf,
                        pltpu.VMEM((LRU_SLABS, max(tm // V7X_SUBLANES, V7X_SUBLANES), V7X_LANES), _F32),
                        lru_buf, pltpu.VMEM((1, LRU_WIDTH), _F32)],
        compiler_params=pltpu.CompilerParams(dimension_semantics=("arbitrary", "arbitrary"),
                                             vmem_limit_bytes=_vmem_limit()),
        name="mixer",
    )(w["attn_sinks"], h1, q, k, v, xb, gb, k0, v0, conv0, hst0,
      w["conv_w"], w["conv_b"], w["lru_wax"], w["lru_ba"], w["lru_bx"], w["lru_lambda"],
      w["w_out"], w["ln2_g"], w["ln2_b"])


def _ffn_out(h2d, p2d, w):
    n = h2d.shape[0]
    tm = _row_tile(n)
    row = lambda width: pl.BlockSpec((tm, width), lambda i: (i, 0))
    return pl.pallas_call(
        _ffn_out_kernel,
        grid=(n // tm,),
        in_specs=[row(D_MODEL), row(PLE_DIM),
                  _resident((D_MODEL, D_FF)), _resident((D_MODEL, D_FF)), _resident((D_FF, D_MODEL)),
                  _resident((1, D_MODEL)), _resident((1, D_MODEL)),
                  _resident((D_MODEL, D_MODEL)), _resident((PLE_DIM, D_MODEL))],
        out_specs=row(D_MODEL),
        out_shape=jax.ShapeDtypeStruct((n, D_MODEL), _F32),
        compiler_params=pltpu.CompilerParams(dimension_semantics=("arbitrary",),
                                             vmem_limit_bytes=_vmem_limit()),
        name="ffn_out",
    )(h2d, p2d, w["ffn2_wg"], w["ffn2_wu"], w["ffn2_wd"], w["ln3_g"], w["ln3_b"], w["w_ple_gate"], w["w_ple"])


def _rope_tables(pos):
    half = ROT_DIM // 2
    inv = jnp.power(jnp.float32(ROPE_THETA), -jnp.arange(half, dtype=_F32) * (2.0 / ROT_DIM))
    ang = pos.astype(_F32)[:, None] * inv[None, :]
    cos, sin = jnp.cos(ang), jnp.sin(ang)
    n = pos.shape[0]
    ident = jnp.ones((n, HEAD_DIM - ROT_DIM), _F32)
    cos_head = jnp.concatenate([cos, cos, ident], axis=1)
    sin_head = jnp.concatenate([-sin, sin, 0.0 * ident], axis=1)
    reps = V7X_LANES // HEAD_DIM
    return jnp.tile(cos_head, (1, reps)), jnp.tile(sin_head, (1, reps))


def _prepare_weights(i, ffn1_wg, ffn1_wu, ffn1_wd, ln1_g, ln1_b, w_in, attn_sinks, conv_w, conv_b,
                     lru_wa, lru_ba, lru_wx, lru_bx, lru_lambda, w_out, ln2_g, ln2_b,
                     ffn2_wg, ffn2_wu, ffn2_wd, ln3_g, ln3_b, w_ple, w_ple_gate):
    row = lambda a: a[i].reshape(1, -1)
    slots = jnp.asarray(_HEAD_SLOTS)
    win = w_in[i]
    wq = win[:, :ATTN_WIDTH].reshape(D_MODEL, N_HEADS, HEAD_DIM)[:, slots].reshape(D_MODEL, ATTN_WIDTH)
    wout = w_out[i]
    wo_attn = wout[:ATTN_WIDTH].reshape(N_HEADS, HEAD_DIM, D_MODEL)[slots].reshape(ATTN_WIDTH, D_MODEL)
    eye = jnp.eye(LRU_BLOCKS, dtype=_F32)
    blockdiag = lambda wb: jnp.einsum("ncd,nm->ncmd", wb, eye).reshape(LRU_WIDTH, LRU_WIDTH)
    return dict(
        ffn1_wg=ffn1_wg[i].astype(_BF16), ffn1_wu=ffn1_wu[i].astype(_BF16), ffn1_wd=ffn1_wd[i].astype(_BF16),
        ln1_g=row(ln1_g), ln1_b=row(ln1_b),
        w_in=jnp.concatenate([wq, win[:, ATTN_WIDTH:]], axis=1).astype(_BF16),
        attn_sinks=attn_sinks[i],
        conv_w=conv_w[i], conv_b=row(conv_b),
        lru_wax=jnp.concatenate([blockdiag(lru_wa[i]), blockdiag(lru_wx[i])], axis=1).astype(_BF16),
        lru_ba=row(lru_ba), lru_bx=row(lru_bx), lru_lambda=row(lru_lambda),
        w_out=jnp.concatenate([wo_attn, wout[ATTN_WIDTH:]], axis=0).astype(_BF16),
        ln2_g=row(ln2_g), ln2_b=row(ln2_b),
        ffn2_wg=ffn2_wg[i].astype(_BF16), ffn2_wu=ffn2_wu[i].astype(_BF16), ffn2_wd=ffn2_wd[i].astype(_BF16),
        ln3_g=row(ln3_g), ln3_b=row(ln3_b),
        w_ple=w_ple[i].astype(_BF16), w_ple_gate=w_ple_gate[i].astype(_BF16))


def _layer(x, p, pos, k_past, v_past, conv_prev, h_prev, w):
    batch, seq = x.shape[0], x.shape[1]
    n = batch * seq
    tm = _row_tile(n)
    cos_tab, sin_tab = _rope_tables(pos)
    if seq < tm:
        cos_tab = jnp.tile(cos_tab, (tm // seq, 1))
        sin_tab = jnp.tile(sin_tab, (tm // seq, 1))
    h1, q, k, v, xb, gb = _ffn_in(x.reshape(n, D_MODEL), cos_tab, sin_tab, w)
    mask_initial_keys = k_past is None
    if mask_initial_keys:
        k0 = jnp.zeros((batch, WINDOW, KV_WIDTH), _F32)
        v0 = jnp.zeros((batch, WINDOW, KV_WIDTH), _F32)
    else:
        k0 = k_past.reshape(batch, WINDOW, KV_WIDTH)
        v0 = v_past.reshape(batch, WINDOW, KV_WIDTH)
    conv0 = jnp.pad(conv_prev, ((0, 0), (V7X_SUBLANES - (CONV_WIDTH - 1), 0), (0, 0)))
    hst0 = h_prev.reshape(batch, 1, LRU_WIDTH)
    h2, new_h = _mixer(h1, q, k, v, xb, gb, k0, v0, conv0, hst0, w,
                       batch=batch, seq=seq, mask_initial_keys=mask_initial_keys)
    y = _ffn_out(h2, p.reshape(n, PLE_DIM), w)
    k_all = jnp.concatenate([k0, k.reshape(batch, seq, KV_WIDTH)], axis=1)[:, -WINDOW:]
    v_all = jnp.concatenate([v0, v.reshape(batch, seq, KV_WIDTH)], axis=1)[:, -WINDOW:]
    conv_all = jnp.concatenate([conv_prev, xb.reshape(batch, seq, LRU_WIDTH)], axis=1)[:, -(CONV_WIDTH - 1):]
    return (y.reshape(batch, seq, D_MODEL),
            k_all.reshape(batch, WINDOW, N_KV_HEADS, HEAD_DIM),
            v_all.reshape(batch, WINDOW, N_KV_HEADS, HEAD_DIM),
            conv_all, new_h.reshape(batch, LRU_WIDTH))


def kernel(x_prompt, x_sample, p_prompt, p_sample, cache_k, cache_v, state_conv, state_h, ffn1_wg, ffn1_wu, ffn1_wd, ln1_g, ln1_b, w_in, attn_sinks, conv_w, conv_b, lru_wa, lru_ba, lru_wx, lru_bx, lru_lambda, w_out, ln2_g, ln2_b, ffn2_wg, ffn2_wu, ffn2_wd, ln3_g, ln3_b, w_ple, w_ple_gate):
    pos_prompt = jnp.arange(x_prompt.shape[1])
    pos_sample = PAST_LEN + jnp.arange(x_sample.shape[1])
    yp, ys = x_prompt, x_sample
    outs_p, outs_s = [], []
    for i in range(DEPTH):
        w = _prepare_weights(i, ffn1_wg, ffn1_wu, ffn1_wd, ln1_g, ln1_b, w_in, attn_sinks, conv_w, conv_b,
                             lru_wa, lru_ba, lru_wx, lru_bx, lru_lambda, w_out, ln2_g, ln2_b,
                             ffn2_wg, ffn2_wu, ffn2_wd, ln3_g, ln3_b, w_ple, w_ple_gate)
        bp = yp.shape[0]
        conv_zero = jnp.zeros((bp, CONV_WIDTH - 1, LRU_WIDTH), yp.dtype)
        h_zero = jnp.zeros((bp, LRU_WIDTH), _F32)
        yp, *rest_p = _layer(yp, p_prompt[i], pos_prompt, None, None, conv_zero, h_zero, w)
        ys, *rest_s = _layer(ys, p_sample[i], pos_sample, cache_k[i], cache_v[i], state_conv[i], state_h[i], w)
        outs_p.append(rest_p)
        outs_s.append(rest_s)
    stack = lambda outs, j: jnp.stack([o[j] for o in outs])
    return (yp, ys,
            stack(outs_p, 0), stack(outs_p, 1), stack(outs_p, 2), stack(outs_p, 3),
            stack(outs_s, 0), stack(outs_s, 1), stack(outs_s, 2), stack(outs_s, 3))
```

```python
import functools

import jax
import jax.numpy as jnp
from jax import lax
from jax.experimental import pallas as pl
from jax.experimental.pallas import tpu as pltpu

D_MODEL = 1024
CHUNK = 64
N_HEADS = 8
N_KV_HEADS = 2
GROUP = N_HEADS // N_KV_HEADS
HEAD_DIM = 64
ATTN_WIDTH = N_HEADS * HEAD_DIM
KV_WIDTH = N_KV_HEADS * HEAD_DIM
WINDOW = 128
ROT_DIM = HEAD_DIM // 4
ROPE_THETA = 500000.0
LRU_WIDTH = D_MODEL - ATTN_WIDTH
LRU_BLOCKS = 8
LRU_BLOCK = LRU_WIDTH // LRU_BLOCKS
CONV_WIDTH = 4
LRU_C = 8.0
IN_COLS = ATTN_WIDTH + 2 * KV_WIDTH + 2 * LRU_WIDTH
D_FF = 2816
PLE_DIM = 256
DEPTH = 1
PAST_LEN = 4096
ALPHA = (2.0 * DEPTH) ** 0.25
LN_EPS = 1e-5
NEG_INF = -1e30

V7X_LANES = 128
V7X_SUBLANES = 8
V7X_VMEM_BYTES = 64 * 1024 * 1024
LRU_SLABS = LRU_WIDTH // V7X_LANES

_HEAD_SLOTS = tuple(h for j in range(GROUP) for h in (j, GROUP + j))

_BF16 = jnp.bfloat16
_F32 = jnp.float32


def _row_parts(rows):
    half = rows // 2
    return (slice(0, half), slice(half, rows))


def _layer_norm(y, g, b):
    mu = jnp.mean(y, -1, keepdims=True)
    var = jnp.mean(jnp.square(y - mu), -1, keepdims=True)
    return (y - mu) * lax.rsqrt(var + LN_EPS) * g + b


def _swiglu(x, wg_ref, wu_ref, wd_ref):
    x16 = x.astype(_BF16)
    gate = jnp.dot(x16, wg_ref[...], preferred_element_type=_F32)
    up = jnp.dot(x16, wu_ref[...], preferred_element_type=_F32)
    act = (jax.nn.silu(gate) * up).astype(_BF16)
    return jnp.dot(act, wd_ref[...], preferred_element_type=_F32)


def _rope_slab(xs, cos, sin_signed, low_half):
    up = pltpu.roll(xs, V7X_LANES - ROT_DIM // 2, axis=1)
    dn = pltpu.roll(xs, ROT_DIM // 2, axis=1)
    return xs * cos + jnp.where(low_half, up, dn) * sin_signed


def _ffn_in_kernel(x_ref, cos_ref, sin_ref, wg_ref, wu_ref, wd_ref, g_ref, b_ref, win_ref,
                   h_ref, q_ref, k_ref, v_ref, xb_ref, gb_ref):
    parts = _row_parts(x_ref.shape[0])
    xs = [x_ref[p, :] for p in parts]
    ffs = [_swiglu(x, wg_ref, wu_ref, wd_ref) for x in xs]
    o1 = ATTN_WIDTH
    o2 = o1 + KV_WIDTH
    o3 = o2 + KV_WIDTH
    o4 = o3 + LRU_WIDTH
    for p, x, ff in zip(parts, xs, ffs):
        h = _layer_norm(ALPHA * x + 0.5 * ff, g_ref[...], b_ref[...])
        h_ref[p, :] = ALPHA * h
        z = jnp.dot(h.astype(_BF16), win_ref[...], preferred_element_type=_F32)
        cos = cos_ref[p, :]
        sin = sin_ref[p, :]
        lane = lax.broadcasted_iota(jnp.int32, cos.shape, 1)
        low_half = (lane & (HEAD_DIM - 1)) < (ROT_DIM // 2)
        for j in range(ATTN_WIDTH // V7X_LANES):
            sl = slice(j * V7X_LANES, (j + 1) * V7X_LANES)
            q_ref[p, sl] = _rope_slab(z[:, sl], cos, sin, low_half).astype(_BF16)
        k_ref[p, :] = _rope_slab(z[:, o1:o2], cos, sin, low_half)
        v_ref[p, :] = z[:, o2:o3]
        for s in range(LRU_SLABS):
            xb_ref[s, p, :] = z[:, o3 + s * V7X_LANES:o3 + (s + 1) * V7X_LANES]
            gb_ref[s, p, :] = z[:, o4 + s * V7X_LANES:o4 + (s + 1) * V7X_LANES]


def _attention(q16, kbuf, vbuf, sinks_ref, first_tile, rows, mask_initial_keys):
    n_chunks = rows // CHUNK
    n_keys = WINDOW + CHUNK
    width = GROUP * CHUNK
    lane = lax.broadcasted_iota(jnp.int32, (CHUNK, V7X_LANES), 1)
    lane_q = lax.broadcasted_iota(jnp.int32, (1, width), 1)
    key_idx = lax.broadcasted_iota(jnp.int32, (n_keys, width), 0)
    sts, sinks = [], []
    for vh in range(N_KV_HEADS):
        sink = jnp.full((1, width), sinks_ref[vh * GROUP], _F32)
        for g in range(1, GROUP):
            sink = jnp.where(lane_q >= g * CHUNK, sinks_ref[vh * GROUP + g], sink)
        sinks.append(sink)
    for c in range(n_chunks):
        kc = kbuf[c * CHUNK:c * CHUNK + n_keys, :]
        qc = q16[c * CHUNK:(c + 1) * CHUNK, :]
        for vh in range(N_KV_HEADS):
            own = (lane >= HEAD_DIM) if vh else (lane < HEAD_DIM)
            qs = [jnp.where(own, qc[:, j * V7X_LANES:(j + 1) * V7X_LANES], jnp.zeros((), _BF16))
                  for j in range(GROUP)]
            q4 = jnp.concatenate(qs, axis=0)
            st = lax.dot_general(kc, q4, (((1,), (1,)), ((), ())), preferred_element_type=_F32)
            if mask_initial_keys and c * CHUNK < WINDOW:
                st = jnp.where(jnp.logical_and(first_tile, key_idx < WINDOW - c * CHUNK), NEG_INF, st)
            sts.append(st)
    s_all = jnp.concatenate(sts, axis=1)
    sink_all = jnp.concatenate(sinks * n_chunks, axis=1)
    m = jnp.maximum(jnp.max(s_all, axis=0, keepdims=True), sink_all)
    pr = jnp.exp(s_all - m)
    denom = jnp.sum(pr, axis=0, keepdims=True) + jnp.exp(sink_all - m)
    w_all = (pr * (1.0 / denom)).astype(_BF16)
    low = lane < HEAD_DIM
    out_chunks = []
    for c in range(n_chunks):
        vc = vbuf[c * CHUNK:c * CHUNK + n_keys, :]
        outs = []
        for vh in range(N_KV_HEADS):
            i = c * N_KV_HEADS + vh
            outs.append(lax.dot_general(w_all[:, i * width:(i + 1) * width], vc, (((0,), (0,)), ((), ())),
                                        preferred_element_type=_F32))
        slabs = [jnp.where(low, outs[0][g * CHUNK:(g + 1) * CHUNK], outs[1][g * CHUNK:(g + 1) * CHUNK])
                 for g in range(GROUP)]
        out_chunks.append(jnp.concatenate(slabs, axis=1))
    return jnp.concatenate(out_chunks, axis=0)


def _lru_scan(a, u, h0, tot_buf, cin_buf, groups):
    n_ph = V7X_SUBLANES
    sup = groups // n_ph
    loc, cum = [u[0]], [a[0]]
    for r in range(1, n_ph):
        loc.append(a[r] * loc[-1] + u[r])
        cum.append(a[r] * cum[-1])
    for s in range(LRU_SLABS):
        lanes = slice(s * V7X_LANES, (s + 1) * V7X_LANES)
        tot_buf[0, s, 0:groups, :] = cum[-1][:, lanes]
        tot_buf[1, s, 0:groups, :] = loc[-1][:, lanes]

    def strided(which, j):
        return jnp.concatenate(
            [tot_buf[which, s, pl.ds(j, sup, stride=n_ph), :] for s in range(LRU_SLABS)], axis=1)

    loc2, cum2 = [strided(1, 0)], [strided(0, 0)]
    for j in range(1, n_ph):
        a2 = strided(0, j)
        loc2.append(a2 * loc2[-1] + strided(1, j))
        cum2.append(a2 * cum2[-1])
    carry = h0
    cin3 = []
    for k in range(sup):
        cin3.append(carry)
        carry = cum2[-1][k:k + 1] * carry + loc2[-1][k:k + 1]
    cin3 = cin3[0] if sup == 1 else jnp.concatenate(cin3, axis=0)
    for j in range(n_ph):
        cin2 = cin3 if j == 0 else cum2[j - 1] * cin3 + loc2[j - 1]
        for s in range(LRU_SLABS):
            cin_buf[s, pl.ds(j, sup, stride=n_ph), :] = cin2[:, s * V7X_LANES:(s + 1) * V7X_LANES]
    cin = jnp.concatenate([cin_buf[s, 0:groups, :] for s in range(LRU_SLABS)], axis=1)
    return [cum[r] * cin + loc[r] for r in range(n_ph)], carry


def _mixer_kernel(sinks_ref, h1_ref, q_ref, k_ref, v_ref, xb_ref, gb_ref,
                  k0_ref, v0_ref, conv0_ref, hst0_ref,
                  convw_ref, convb_ref, wax_ref, ba_ref, bx_ref, lam_ref,
                  wout_ref, g_ref, b_ref,
                  h2_ref, hst_ref,
                  kbuf, vbuf, sbuf, tot_buf, cin_buf, lru_buf, hcar,
                  *, rows, mask_initial_keys):
    t = pl.program_id(1)
    n_ph = V7X_SUBLANES
    groups = rows // n_ph
    tail = CONV_WIDTH - 1

    @pl.when(t == 0)
    def _():
        kbuf[0:WINDOW, :] = k0_ref[...].astype(_BF16)
        vbuf[0:WINDOW, :] = v0_ref[...].astype(_BF16)
        for s in range(LRU_SLABS):
            sbuf[s, 0:n_ph, :] = conv0_ref[:, s * V7X_LANES:(s + 1) * V7X_LANES]
        hcar[...] = hst0_ref[...]

    kbuf[WINDOW:WINDOW + rows, :] = k_ref[...].astype(_BF16)
    vbuf[WINDOW:WINDOW + rows, :] = v_ref[...].astype(_BF16)

    attn = _attention(q_ref[...], kbuf, vbuf, sinks_ref, t == 0, rows, mask_initial_keys)

    def phase(ref, r):
        return jnp.concatenate(
            [ref[s, pl.ds(r, groups, stride=n_ph), :] for s in range(LRU_SLABS)], axis=1)

    xph = [phase(xb_ref, r) for r in range(n_ph)]
    prev = {}
    for r in range(n_ph - tail, n_ph):
        for s in range(LRU_SLABS):
            sbuf[s, pl.ds(n_ph + r, groups, stride=n_ph), :] = xph[r][:, s * V7X_LANES:(s + 1) * V7X_LANES]
        prev[r - n_ph] = phase(sbuf, r)
    for s in range(LRU_SLABS):
        sbuf[s, n_ph - tail:n_ph, :] = sbuf[s, rows + n_ph - tail:rows + n_ph, :]
    cw = convw_ref[...]
    xcs = []
    for r in range(n_ph):
        tap = lambda i: xph[i] if i >= 0 else prev[i]
        xc = convb_ref[...] + cw[0:1, :] * tap(r - tail)
        for j in range(1, CONV_WIDTH):
            xc = xc + cw[j:j + 1, :] * tap(r - tail + j)
        xcs.append(xc)
    xc_all = jnp.concatenate(xcs, axis=0)
    gates = jnp.dot(xc_all.astype(_BF16), wax_ref[...], preferred_element_type=_F32)
    neg_c = -LRU_C * jax.nn.softplus(-lam_ref[...])
    a_ph, u_ph = [], []
    for r in range(n_ph):
        gr = gates[r * groups:(r + 1) * groups]
        r_gate = jax.nn.sigmoid(gr[:, :LRU_WIDTH] + ba_ref[...])
        i_gate = jax.nn.sigmoid(gr[:, LRU_WIDTH:] + bx_ref[...])
        log_a = neg_c * r_gate
        a_gate = jnp.exp(log_a)
        one_minus_a2 = -jnp.tanh(log_a) * (a_gate * a_gate + 1.0)
        a_ph.append(a_gate)
        u_ph.append(jnp.sqrt(one_minus_a2) * (i_gate * xcs[r]))
    hs, last = _lru_scan(a_ph, u_ph, hcar[...], tot_buf, cin_buf, groups)
    hcar[...] = last
    hst_ref[...] = last
    for r in range(n_ph):
        lru_r = hs[r] * jax.nn.gelu(phase(gb_ref, r))
        for s in range(LRU_SLABS):
            lru_buf[s, pl.ds(r, groups, stride=n_ph), :] = lru_r[:, s * V7X_LANES:(s + 1) * V7X_LANES]
    lru = jnp.concatenate([lru_buf[s] for s in range(LRU_SLABS)], axis=1)

    cat = jnp.concatenate([attn, lru], axis=1).astype(_BF16)
    parts = _row_parts(rows)
    mixes = [jnp.dot(cat[p, :], wout_ref[...], preferred_element_type=_F32) for p in parts]
    for p, mix in zip(parts, mixes):
        h2_ref[p, :] = _layer_norm(h1_ref[p, :] + mix, g_ref[...], b_ref[...])

    kbuf[0:WINDOW, :] = kbuf[rows:rows + WINDOW, :]
    vbuf[0:WINDOW, :] = vbuf[rows:rows + WINDOW, :]


def _ffn_out_kernel(h_ref, p_ref, wg_ref, wu_ref, wd_ref, g_ref, b_ref, wgate_ref, wple_ref, y_ref):
    parts = _row_parts(h_ref.shape[0])
    xs = [h_ref[p, :] for p in parts]
    ffs = [_swiglu(x, wg_ref, wu_ref, wd_ref) for x in xs]
    for p, x, ff in zip(parts, xs, ffs):
        h = _layer_norm(ALPHA * x + 0.5 * ff, g_ref[...], b_ref[...])
        gate = jax.nn.sigmoid(jnp.dot(h.astype(_BF16), wgate_ref[...], preferred_element_type=_F32))
        ple = jnp.dot(p_ref[p, :].astype(_BF16), wple_ref[...], preferred_element_type=_F32)
        y_ref[p, :] = h + gate * ple


def _resident(shape):
    zeros = (0,) * len(shape)
    return pl.BlockSpec(shape, lambda *_: zeros, pipeline_mode=pl.Buffered(1))


def _row_tile(total_rows):
    return min(512, total_rows)


def _vmem_limit():
    return V7X_VMEM_BYTES - 6 * 1024 * 1024


def _ffn_in(x2d, cos_tab, sin_tab, w):
    n = x2d.shape[0]
    tm = _row_tile(n)
    tab_tiles = cos_tab.shape[0] // tm
    row = lambda width: pl.BlockSpec((tm, width), lambda i: (i, 0))
    tab = pl.BlockSpec((tm, V7X_LANES), lambda i: (i % tab_tiles, 0))
    slab = pl.BlockSpec((LRU_SLABS, tm, V7X_LANES), lambda i: (0, i, 0))
    slab_shape = jax.ShapeDtypeStruct((LRU_SLABS, n, V7X_LANES), _F32)
    return pl.pallas_call(
        _ffn_in_kernel,
        grid=(n // tm,),
        in_specs=[row(D_MODEL), tab, tab,
                  _resident((D_MODEL, D_FF)), _resident((D_MODEL, D_FF)), _resident((D_FF, D_MODEL)),
                  _resident((1, D_MODEL)), _resident((1, D_MODEL)), _resident((D_MODEL, IN_COLS))],
        out_specs=[row(D_MODEL), row(ATTN_WIDTH), row(KV_WIDTH), row(KV_WIDTH), slab, slab],
        out_shape=[jax.ShapeDtypeStruct((n, D_MODEL), _F32), jax.ShapeDtypeStruct((n, ATTN_WIDTH), _BF16),
                   jax.ShapeDtypeStruct((n, KV_WIDTH), _F32), jax.ShapeDtypeStruct((n, KV_WIDTH), _F32),
                   slab_shape, slab_shape],
        compiler_params=pltpu.CompilerParams(dimension_semantics=("arbitrary",),
                                             vmem_limit_bytes=_vmem_limit()),
        name="ffn_in",
    )(x2d, cos_tab, sin_tab, w["ffn1_wg"], w["ffn1_wu"], w["ffn1_wd"], w["ln1_g"], w["ln1_b"], w["w_in"])


def _mixer(h1a, q, k, v, xb, gb, k0, v0, conv0, hst0, w, *, batch, seq, mask_initial_keys):
    tm = _row_tile(seq)
    nt = seq // tm
    groups = tm // V7X_SUBLANES
    row = lambda width: pl.BlockSpec((tm, width), lambda b, t: (b * nt + t, 0))
    slab = pl.BlockSpec((LRU_SLABS, tm, V7X_LANES), lambda b, t: (0, b * nt + t, 0))
    per_batch = lambda r, c: pl.BlockSpec((None, r, c), lambda b, t: (b, 0, 0))
    kernel = functools.partial(_mixer_kernel, rows=tm, mask_initial_keys=mask_initial_keys)
    return pl.pallas_call(
        kernel,
        grid=(batch, nt),
        in_specs=[pl.BlockSpec(memory_space=pltpu.SMEM),
                  row(D_MODEL), row(ATTN_WIDTH), row(KV_WIDTH), row(KV_WIDTH), slab, slab,
                  per_batch(WINDOW, KV_WIDTH), per_batch(WINDOW, KV_WIDTH),
                  per_batch(V7X_SUBLANES, LRU_WIDTH), per_batch(1, LRU_WIDTH),
                  _resident((CONV_WIDTH, LRU_WIDTH)), _resident((1, LRU_WIDTH)),
                  _resident((LRU_WIDTH, 2 * LRU_WIDTH)), _resident((1, LRU_WIDTH)), _resident((1, LRU_WIDTH)),
                  _resident((1, LRU_WIDTH)),
                  _resident((D_MODEL, D_MODEL)), _resident((1, D_MODEL)), _resident((1, D_MODEL))],
        out_specs=[row(D_MODEL), per_batch(1, LRU_WIDTH)],
        out_shape=[jax.ShapeDtypeStruct((batch * seq, D_MODEL), _F32),
                   jax.ShapeDtypeStruct((batch, 1, LRU_WIDTH), _F32)],
        scratch_shapes=[pltpu.VMEM((WINDOW + tm, KV_WIDTH), _BF16), pltpu.VMEM((WINDOW + tm, KV_WIDTH), _BF16),
                        pltpu.VMEM((LRU_SLABS, V7X_SUBLANES + tm, V7X_LANES), _F32),
                        pltpu.VMEM((2, LRU_SLABS, groups, V7X_LANES), _F32),
                        pltpu.VMEM((LRU_SLABS, groups, V7X_LANES), _F32),
                        pltpu.VMEM((LRU_SLABS, tm, V7X_LANES), _F32),
                        pltpu.VMEM((1, LRU_WIDTH), _F32)],
        compiler_params=pltpu.CompilerParams(dimension_semantics=("arbitrary", "arbitrary"),
                                             vmem_limit_bytes=_vmem_limit()),
        name="mixer",
    )(w["attn_sinks"], h1a, q, k, v, xb, gb, k0, v0, conv0, hst0,
      w["conv_w"], w["conv_b"], w["lru_wax"], w["lru_ba"], w["lru_bx"], w["lru_lambda"],
      w["w_out"], w["ln2_g"], w["ln2_b"])


def _ffn_out(h2d, p2d, w):
    n = h2d.shape[0]
    tm = _row_tile(n)
    row = lambda width: pl.BlockSpec((tm, width), lambda i: (i, 0))
    return pl.pallas_call(
        _ffn_out_kernel,
        grid=(n // tm,),
        in_specs=[row(D_MODEL), row(PLE_DIM),
                  _resident((D_MODEL, D_FF)), _resident((D_MODEL, D_FF)), _resident((D_FF, D_MODEL)),
                  _resident((1, D_MODEL)), _resident((1, D_MODEL)),
                  _resident((D_MODEL, D_MODEL)), _resident((PLE_DIM, D_MODEL))],
        out_specs=row(D_MODEL),
        out_shape=jax.ShapeDtypeStruct((n, D_MODEL), _F32),
        compiler_params=pltpu.CompilerParams(dimension_semantics=("arbitrary",),
                                             vmem_limit_bytes=_vmem_limit()),
        name="ffn_out",
    )(h2d, p2d, w["ffn2_wg"], w["ffn2_wu"], w["ffn2_wd"], w["ln3_g"], w["ln3_b"], w["w_ple_gate"], w["w_ple"])


def _rope_tables(pos):
    half = ROT_DIM // 2
    inv = jnp.power(jnp.float32(ROPE_THETA), -jnp.arange(half, dtype=_F32) * (2.0 / ROT_DIM))
    ang = pos.astype(_F32)[:, None] * inv[None, :]
    cos, sin = jnp.cos(ang), jnp.sin(ang)
    n = pos.shape[0]
    ident = jnp.ones((n, HEAD_DIM - ROT_DIM), _F32)
    cos_head = jnp.concatenate([cos, cos, ident], axis=1)
    sin_head = jnp.concatenate([-sin, sin, 0.0 * ident], axis=1)
    reps = V7X_LANES // HEAD_DIM
    return jnp.tile(cos_head, (1, reps)), jnp.tile(sin_head, (1, reps))


def _prepare_weights(i, ffn1_wg, ffn1_wu, ffn1_wd, ln1_g, ln1_b, w_in, attn_sinks, conv_w, conv_b,
                     lru_wa, lru_ba, lru_wx, lru_bx, lru_lambda, w_out, ln2_g, ln2_b,
                     ffn2_wg, ffn2_wu, ffn2_wd, ln3_g, ln3_b, w_ple, w_ple_gate):
    row = lambda a: a[i].reshape(1, -1)
    slots = jnp.asarray(_HEAD_SLOTS)
    win = w_in[i]
    wq = win[:, :ATTN_WIDTH].reshape(D_MODEL, N_HEADS, HEAD_DIM)[:, slots].reshape(D_MODEL, ATTN_WIDTH)
    wq = wq * (HEAD_DIM ** -0.5)
    wout = w_out[i]
    wo_attn = wout[:ATTN_WIDTH].reshape(N_HEADS, HEAD_DIM, D_MODEL)[slots].reshape(ATTN_WIDTH, D_MODEL)
    eye = jnp.eye(LRU_BLOCKS, dtype=_F32)
    blockdiag = lambda wb: jnp.einsum("ncd,nm->ncmd", wb, eye).reshape(LRU_WIDTH, LRU_WIDTH)
    return dict(
        ffn1_wg=ffn1_wg[i].astype(_BF16), ffn1_wu=ffn1_wu[i].astype(_BF16), ffn1_wd=ffn1_wd[i].astype(_BF16),
        ln1_g=row(ln1_g), ln1_b=row(ln1_b),
        w_in=jnp.concatenate([wq, win[:, ATTN_WIDTH:]], axis=1).astype(_BF16),
        attn_sinks=attn_sinks[i],
        conv_w=conv_w[i], conv_b=row(conv_b),
        lru_wax=jnp.concatenate([blockdiag(lru_wa[i]), blockdiag(lru_wx[i])], axis=1).astype(_BF16),
        lru_ba=row(lru_ba), lru_bx=row(lru_bx), lru_lambda=row(lru_lambda),
        w_out=jnp.concatenate([wo_attn, wout[ATTN_WIDTH:]], axis=0).astype(_BF16),
        ln2_g=row(ln2_g), ln2_b=row(ln2_b),
        ffn2_wg=ffn2_wg[i].astype(_BF16), ffn2_wu=ffn2_wu[i].astype(_BF16), ffn2_wd=ffn2_wd[i].astype(_BF16),
        ln3_g=row(ln3_g), ln3_b=row(ln3_b),
        w_ple=w_ple[i].astype(_BF16), w_ple_gate=w_ple_gate[i].astype(_BF16))


def _layer(x, p, pos, k_past, v_past, conv_prev, h_prev, w):
    batch, seq = x.shape[0], x.shape[1]
    n = batch * seq
    tm = _row_tile(n)
    cos_tab, sin_tab = _rope_tables(pos)
    if seq < tm:
        cos_tab = jnp.tile(cos_tab, (tm // seq, 1))
        sin_tab = jnp.tile(sin_tab, (tm // seq, 1))
    h1a, q, k, v, xb, gb = _ffn_in(x.reshape(n, D_MODEL), cos_tab, sin_tab, w)
    mask_initial_keys = k_past is None
    if mask_initial_keys:
        k0 = jnp.zeros((batch, WINDOW, KV_WIDTH), _F32)
        v0 = jnp.zeros((batch, WINDOW, KV_WIDTH), _F32)
    else:
        k0 = k_past.reshape(batch, WINDOW, KV_WIDTH)
        v0 = v_past.reshape(batch, WINDOW, KV_WIDTH)
    conv0 = jnp.pad(conv_prev, ((0, 0), (V7X_SUBLANES - (CONV_WIDTH - 1), 0), (0, 0)))
    hst0 = h_prev.reshape(batch, 1, LRU_WIDTH)
    h2, new_h = _mixer(h1a, q, k, v, xb, gb, k0, v0, conv0, hst0, w,
                       batch=batch, seq=seq, mask_initial_keys=mask_initial_keys)
    y = _ffn_out(h2, p.reshape(n, PLE_DIM), w)
    k_all = jnp.concatenate([k0, k.reshape(batch, seq, KV_WIDTH)], axis=1)[:, -WINDOW:]
    v_all = jnp.concatenate([v0, v.reshape(batch, seq, KV_WIDTH)], axis=1)[:, -WINDOW:]
    xb_tail = xb.reshape(LRU_SLABS, batch, seq, V7X_LANES)[:, :, -(CONV_WIDTH - 1):]
    xb_tail = xb_tail.transpose(1, 2, 0, 3).reshape(batch, -1, LRU_WIDTH)
    conv_all = jnp.concatenate([conv_prev, xb_tail], axis=1)[:, -(CONV_WIDTH - 1):]
    return (y.reshape(batch, seq, D_MODEL),
            k_all.reshape(batch, WINDOW, N_KV_HEADS, HEAD_DIM),
            v_all.reshape(batch, WINDOW, N_KV_HEADS, HEAD_DIM),
            conv_all, new_h.reshape(batch, LRU_WIDTH))


def kernel(x_prompt, x_sample, p_prompt, p_sample, cache_k, cache_v, state_conv, state_h, ffn1_wg, ffn1_wu, ffn1_wd, ln1_g, ln1_b, w_in, attn_sinks, conv_w, conv_b, lru_wa, lru_ba, lru_wx, lru_bx, lru_lambda, w_out, ln2_g, ln2_b, ffn2_wg, ffn2_wu, ffn2_wd, ln3_g, ln3_b, w_ple, w_ple_gate):
    pos_prompt = jnp.arange(x_prompt.shape[1])
    pos_sample = PAST_LEN + jnp.arange(x_sample.shape[1])
    yp, ys = x_prompt, x_sample
    outs_p, outs_s = [], []
    for i in range(DEPTH):
        w = _prepare_weights(i, ffn1_wg, ffn1_wu, ffn1_wd, ln1_g, ln1_b, w_in, attn_sinks, conv_w, conv_b,
                             lru_wa, lru_ba, lru_wx, lru_bx, lru_lambda, w_out, ln2_g, ln2_b,
                             ffn2_wg, ffn2_wu, ffn2_wd, ln3_g, ln3_b, w_ple, w_ple_gate)
        bp = yp.shape[0]
        conv_zero = jnp.zeros((bp, CONV_WIDTH - 1, LRU_WIDTH), yp.dtype)
        h_zero = jnp.zeros((bp, LRU_WIDTH), _F32)
        yp, *rest_p = _layer(yp, p_prompt[i], pos_prompt, None, None, conv_zero, h_zero, w)
        ys, *rest_s = _layer(ys, p_sample[i], pos_sample, cache_k[i], cache_v[i], state_conv[i], state_h[i], w)
        outs_p.append(rest_p)
        outs_s.append(rest_s)
    stack = lambda outs, j: jnp.stack([o[j] for o in outs])
    return (yp, ys,
            stack(outs_p, 0), stack(outs_p, 1), stack(outs_p, 2), stack(outs_p, 3),
            stack(outs_s, 0), stack(outs_s, 1), stack(outs_s, 2), stack(outs_s, 3))
```

```python
import functools

import jax
import jax.numpy as jnp
import numpy as np
from jax import lax
from jax.experimental import pallas as pl
from jax.experimental.pallas import tpu as pltpu

D_MODEL = 1024
CHUNK = 64
N_HEADS = 8
N_KV_HEADS = 2
GROUP = N_HEADS // N_KV_HEADS
HEAD_DIM = 64
ATTN_WIDTH = N_HEADS * HEAD_DIM
KV_WIDTH = N_KV_HEADS * HEAD_DIM
WINDOW = 128
ROT_DIM = HEAD_DIM // 4
ROPE_THETA = 500000.0
LRU_WIDTH = D_MODEL - ATTN_WIDTH
LRU_BLOCKS = 8
LRU_BLOCK = LRU_WIDTH // LRU_BLOCKS
CONV_WIDTH = 4
LRU_C = 8.0
IN_COLS = ATTN_WIDTH + 2 * KV_WIDTH + 2 * LRU_WIDTH
D_FF = 2816
PLE_DIM = 256
DEPTH = 1
PAST_LEN = 4096
ALPHA = (2.0 * DEPTH) ** 0.25
LN_EPS = 1e-5
NEG_INF = -1e30

V7X_LANES = 128
V7X_SUBLANES = 8
V7X_VMEM_BYTES = 64 * 1024 * 1024
LRU_SLABS = LRU_WIDTH // V7X_LANES

_HEAD_SLOTS = tuple(h for j in range(GROUP) for h in (j, GROUP + j))

_BF16 = jnp.bfloat16
_F32 = jnp.float32


def _row_parts(rows):
    half = rows // 2
    return (slice(0, half), slice(half, rows))


def _layer_norm(y, g, b):
    mu = jnp.mean(y, -1, keepdims=True)
    var = jnp.mean(jnp.square(y - mu), -1, keepdims=True)
    return (y - mu) * lax.rsqrt(var + LN_EPS) * g + b


def _swiglu(x, wg_ref, wu_ref, wd_ref):
    x16 = x.astype(_BF16)
    gate = jnp.dot(x16, wg_ref[...], preferred_element_type=_F32)
    up = jnp.dot(x16, wu_ref[...], preferred_element_type=_F32)
    act = (jax.nn.silu(gate) * up).astype(_BF16)
    return jnp.dot(act, wd_ref[...], preferred_element_type=_F32)


def _rope_slab(xs, cos, sin_signed, low_half):
    up = pltpu.roll(xs, V7X_LANES - ROT_DIM // 2, axis=1)
    dn = pltpu.roll(xs, ROT_DIM // 2, axis=1)
    return xs * cos + jnp.where(low_half, up, dn) * sin_signed


def _ffn_in_kernel(x_ref, cos_ref, sin_ref, wg_ref, wu_ref, wd_ref, g_ref, b_ref, win_ref,
                   h_ref, q_ref, k_ref, v_ref, xb_ref, gb_ref):
    parts = _row_parts(x_ref.shape[0])
    xs = [x_ref[p, :] for p in parts]
    ffs = [_swiglu(x, wg_ref, wu_ref, wd_ref) for x in xs]
    o1 = ATTN_WIDTH
    o2 = o1 + KV_WIDTH
    o3 = o2 + KV_WIDTH
    o4 = o3 + LRU_WIDTH
    for p, x, ff in zip(parts, xs, ffs):
        h = _layer_norm(ALPHA * x + 0.5 * ff, g_ref[...], b_ref[...])
        h_ref[p, :] = ALPHA * h
        z = jnp.dot(h.astype(_BF16), win_ref[...], preferred_element_type=_F32)
        cos = cos_ref[p, :]
        sin = sin_ref[p, :]
        lane = lax.broadcasted_iota(jnp.int32, cos.shape, 1)
        low_half = (lane & (HEAD_DIM - 1)) < (ROT_DIM // 2)
        for j in range(ATTN_WIDTH // V7X_LANES):
            sl = slice(j * V7X_LANES, (j + 1) * V7X_LANES)
            q_ref[p, sl] = _rope_slab(z[:, sl], cos, sin, low_half).astype(_BF16)
        k_ref[p, :] = _rope_slab(z[:, o1:o2], cos, sin, low_half)
        v_ref[p, :] = z[:, o2:o3]
        for s in range(LRU_SLABS):
            xb_ref[s, p, :] = z[:, o3 + s * V7X_LANES:o3 + (s + 1) * V7X_LANES]
            gb_ref[s, p, :] = z[:, o4 + s * V7X_LANES:o4 + (s + 1) * V7X_LANES]


def _attention(q16, kbuf, vbuf, sinks_ref, first_tile, rows, mask_initial_keys):
    n_chunks = rows // CHUNK
    n_keys = WINDOW + CHUNK
    width = GROUP * CHUNK
    lane = lax.broadcasted_iota(jnp.int32, (CHUNK, V7X_LANES), 1)
    lane_q = lax.broadcasted_iota(jnp.int32, (1, width), 1)
    key_idx = lax.broadcasted_iota(jnp.int32, (n_keys, width), 0)
    sts, sinks = [], []
    for vh in range(N_KV_HEADS):
        sink = jnp.full((1, width), sinks_ref[vh * GROUP], _F32)
        for g in range(1, GROUP):
            sink = jnp.where(lane_q >= g * CHUNK, sinks_ref[vh * GROUP + g], sink)
        sinks.append(sink)
    for c in range(n_chunks):
        kc = kbuf[c * CHUNK:c * CHUNK + n_keys, :]
        qc = q16[c * CHUNK:(c + 1) * CHUNK, :]
        for vh in range(N_KV_HEADS):
            own = (lane >= HEAD_DIM) if vh else (lane < HEAD_DIM)
            qs = [jnp.where(own, qc[:, j * V7X_LANES:(j + 1) * V7X_LANES], jnp.zeros((), _BF16))
                  for j in range(GROUP)]
            q4 = jnp.concatenate(qs, axis=0)
            st = lax.dot_general(kc, q4, (((1,), (1,)), ((), ())), preferred_element_type=_F32)
            if mask_initial_keys and c * CHUNK < WINDOW:
                st = jnp.where(jnp.logical_and(first_tile, key_idx < WINDOW - c * CHUNK), NEG_INF, st)
            sts.append(st)
    s_all = jnp.concatenate(sts, axis=1)
    sink_all = jnp.concatenate(sinks * n_chunks, axis=1)
    m = jnp.maximum(jnp.max(s_all, axis=0, keepdims=True), sink_all)
    pr = jnp.exp(s_all - m)
    denom = jnp.sum(pr, axis=0, keepdims=True) + jnp.exp(sink_all - m)
    w_all = (pr * (1.0 / denom)).astype(_BF16)
    low = lane < HEAD_DIM
    out_chunks = []
    for c in range(n_chunks):
        vc = vbuf[c * CHUNK:c * CHUNK + n_keys, :]
        outs = []
        for vh in range(N_KV_HEADS):
            i = c * N_KV_HEADS + vh
            outs.append(lax.dot_general(w_all[:, i * width:(i + 1) * width], vc, (((0,), (0,)), ((), ())),
                                        preferred_element_type=_F32))
        slabs = [jnp.where(low, outs[0][g * CHUNK:(g + 1) * CHUNK], outs[1][g * CHUNK:(g + 1) * CHUNK])
                 for g in range(GROUP)]
        out_chunks.append(jnp.concatenate(slabs, axis=1))
    return jnp.concatenate(out_chunks, axis=0)


def _lru_scan(a, u, h0, tot_buf, cin_buf, groups):
    n_ph = V7X_SUBLANES
    sup = groups // n_ph
    loc, cum = [u[0]], [a[0]]
    for r in range(1, n_ph):
        loc.append(a[r] * loc[-1] + u[r])
        cum.append(a[r] * cum[-1])
    for s in range(LRU_SLABS):
        lanes = slice(s * V7X_LANES, (s + 1) * V7X_LANES)
        tot_buf[0, s, 0:groups, :] = cum[-1][:, lanes]
        tot_buf[1, s, 0:groups, :] = loc[-1][:, lanes]

    def strided(which, j):
        return jnp.concatenate(
            [tot_buf[which, s, pl.ds(j, sup, stride=n_ph), :] for s in range(LRU_SLABS)], axis=1)

    loc2, cum2 = [strided(1, 0)], [strided(0, 0)]
    for j in range(1, n_ph):
        a2 = strided(0, j)
        loc2.append(a2 * loc2[-1] + strided(1, j))
        cum2.append(a2 * cum2[-1])
    carry = h0
    cin3 = []
    for k in range(sup):
        cin3.append(carry)
        carry = cum2[-1][k:k + 1] * carry + loc2[-1][k:k + 1]
    cin3 = cin3[0] if sup == 1 else jnp.concatenate(cin3, axis=0)
    for j in range(n_ph):
        cin2 = cin3 if j == 0 else cum2[j - 1] * cin3 + loc2[j - 1]
        for s in range(LRU_SLABS):
            cin_buf[s, pl.ds(j, sup, stride=n_ph), :] = cin2[:, s * V7X_LANES:(s + 1) * V7X_LANES]
    cin = jnp.concatenate([cin_buf[s, 0:groups, :] for s in range(LRU_SLABS)], axis=1)
    return [cum[r] * cin + loc[r] for r in range(n_ph)], carry


def _mixer_kernel(sinks_ref, h1_ref, q_ref, k_ref, v_ref, xb_ref, gb_ref,
                  k0_ref, v0_ref, conv0_ref, hst0_ref,
                  convw_ref, convb_ref, wax_ref, ba_ref, bx_ref, lam_ref,
                  wout_ref, g_ref, b_ref,
                  h2_ref, hst_ref,
                  kbuf, vbuf, sbuf, tot_buf, cin_buf, lru_buf, hcar,
                  *, rows, mask_initial_keys):
    t = pl.program_id(1)
    n_ph = V7X_SUBLANES
    groups = rows // n_ph
    tail = CONV_WIDTH - 1

    @pl.when(t == 0)
    def _():
        kbuf[0:WINDOW, :] = k0_ref[...].astype(_BF16)
        vbuf[0:WINDOW, :] = v0_ref[...].astype(_BF16)
        for s in range(LRU_SLABS):
            sbuf[s, 0:n_ph, :] = conv0_ref[:, s * V7X_LANES:(s + 1) * V7X_LANES]
        hcar[...] = hst0_ref[...]

    kbuf[WINDOW:WINDOW + rows, :] = k_ref[...].astype(_BF16)
    vbuf[WINDOW:WINDOW + rows, :] = v_ref[...].astype(_BF16)

    attn = _attention(q_ref[...], kbuf, vbuf, sinks_ref, t == 0, rows, mask_initial_keys)

    def phase(ref, r):
        return jnp.concatenate(
            [ref[s, pl.ds(r, groups, stride=n_ph), :] for s in range(LRU_SLABS)], axis=1)

    xph = [phase(xb_ref, r) for r in range(n_ph)]
    prev = {}
    for r in range(n_ph - tail, n_ph):
        for s in range(LRU_SLABS):
            sbuf[s, pl.ds(n_ph + r, groups, stride=n_ph), :] = xph[r][:, s * V7X_LANES:(s + 1) * V7X_LANES]
        prev[r - n_ph] = phase(sbuf, r)
    for s in range(LRU_SLABS):
        sbuf[s, n_ph - tail:n_ph, :] = sbuf[s, rows + n_ph - tail:rows + n_ph, :]
    cw = convw_ref[...]
    xcs = []
    for r in range(n_ph):
        tap = lambda i: xph[i] if i >= 0 else prev[i]
        xc = convb_ref[...] + cw[0:1, :] * tap(r - tail)
        for j in range(1, CONV_WIDTH):
            xc = xc + cw[j:j + 1, :] * tap(r - tail + j)
        xcs.append(xc)
    xc_all = jnp.concatenate(xcs, axis=0)
    gates = jnp.dot(xc_all.astype(_BF16), wax_ref[...], preferred_element_type=_F32)
    neg_c = -LRU_C * jax.nn.softplus(-lam_ref[...])
    a_ph, u_ph = [], []
    for r in range(n_ph):
        gr = gates[r * groups:(r + 1) * groups]
        r_gate = jax.nn.sigmoid(gr[:, :LRU_WIDTH] + ba_ref[...])
        i_gate = jax.nn.sigmoid(gr[:, LRU_WIDTH:] + bx_ref[...])
        log_a = neg_c * r_gate
        a_gate = jnp.exp(log_a)
        one_minus_a2 = -jnp.tanh(log_a) * (a_gate * a_gate + 1.0)
        a_ph.append(a_gate)
        u_ph.append(jnp.sqrt(one_minus_a2) * (i_gate * xcs[r]))
    hs, last = _lru_scan(a_ph, u_ph, hcar[...], tot_buf, cin_buf, groups)
    hcar[...] = last
    hst_ref[...] = last
    for r in range(n_ph):
        lru_r = hs[r] * jax.nn.gelu(phase(gb_ref, r))
        for s in range(LRU_SLABS):
            lru_buf[s, pl.ds(r, groups, stride=n_ph), :] = lru_r[:, s * V7X_LANES:(s + 1) * V7X_LANES]
    lru = jnp.concatenate([lru_buf[s] for s in range(LRU_SLABS)], axis=1)

    cat = jnp.concatenate([attn, lru], axis=1).astype(_BF16)
    parts = _row_parts(rows)
    mixes = [jnp.dot(cat[p, :], wout_ref[...], preferred_element_type=_F32) for p in parts]
    for p, mix in zip(parts, mixes):
        h2_ref[p, :] = _layer_norm(h1_ref[p, :] + mix, g_ref[...], b_ref[...])

    kbuf[0:WINDOW, :] = kbuf[rows:rows + WINDOW, :]
    vbuf[0:WINDOW, :] = vbuf[rows:rows + WINDOW, :]


def _ffn_out_kernel(h_ref, p_ref, wg_ref, wu_ref, wd_ref, g_ref, b_ref, wgate_ref, wple_ref, y_ref):
    parts = _row_parts(h_ref.shape[0])
    xs = [h_ref[p, :] for p in parts]
    ffs = [_swiglu(x, wg_ref, wu_ref, wd_ref) for x in xs]
    for p, x, ff in zip(parts, xs, ffs):
        h = _layer_norm(ALPHA * x + 0.5 * ff, g_ref[...], b_ref[...])
        gate = jax.nn.sigmoid(jnp.dot(h.astype(_BF16), wgate_ref[...], preferred_element_type=_F32))
        ple = jnp.dot(p_ref[p, :].astype(_BF16), wple_ref[...], preferred_element_type=_F32)
        y_ref[p, :] = h + gate * ple


def _resident(shape):
    zeros = (0,) * len(shape)
    return pl.BlockSpec(shape, lambda *_: zeros, pipeline_mode=pl.Buffered(1))


def _row_tile(total_rows):
    return min(512, total_rows)


def _mixer_tile(seq):
    return min(1024, seq)


def _vmem_limit():
    return V7X_VMEM_BYTES - 6 * 1024 * 1024


def _ffn_in(x2d, cos_tab, sin_tab, w):
    n = x2d.shape[0]
    tm = _row_tile(n)
    tab_tiles = cos_tab.shape[0] // tm
    row = lambda width: pl.BlockSpec((tm, width), lambda i: (i, 0))
    tab = pl.BlockSpec((tm, V7X_LANES), lambda i: (i % tab_tiles, 0))
    slab = pl.BlockSpec((LRU_SLABS, tm, V7X_LANES), lambda i: (0, i, 0))
    slab_shape = jax.ShapeDtypeStruct((LRU_SLABS, n, V7X_LANES), _F32)
    return pl.pallas_call(
        _ffn_in_kernel,
        grid=(n // tm,),
        in_specs=[row(D_MODEL), tab, tab,
                  _resident((D_MODEL, D_FF)), _resident((D_MODEL, D_FF)), _resident((D_FF, D_MODEL)),
                  _resident((1, D_MODEL)), _resident((1, D_MODEL)), _resident((D_MODEL, IN_COLS))],
        out_specs=[row(D_MODEL), row(ATTN_WIDTH), row(KV_WIDTH), row(KV_WIDTH), slab, slab],
        out_shape=[jax.ShapeDtypeStruct((n, D_MODEL), _F32), jax.ShapeDtypeStruct((n, ATTN_WIDTH), _BF16),
                   jax.ShapeDtypeStruct((n, KV_WIDTH), _F32), jax.ShapeDtypeStruct((n, KV_WIDTH), _F32),
                   slab_shape, slab_shape],
        compiler_params=pltpu.CompilerParams(dimension_semantics=("arbitrary",),
                                             vmem_limit_bytes=_vmem_limit()),
        name="ffn_in",
    )(x2d, cos_tab, sin_tab, w["ffn1_wg"], w["ffn1_wu"], w["ffn1_wd"], w["ln1_g"], w["ln1_b"], w["w_in"])


def _mixer(h1a, q, k, v, xb, gb, k0, v0, conv0, hst0, w, *, batch, seq, mask_initial_keys):
    tm = _mixer_tile(seq)
    nt = seq // tm
    groups = tm // V7X_SUBLANES
    row = lambda width: pl.BlockSpec((tm, width), lambda b, t: (b * nt + t, 0))
    slab = pl.BlockSpec((LRU_SLABS, tm, V7X_LANES), lambda b, t: (0, b * nt + t, 0))
    per_batch = lambda r, c: pl.BlockSpec((None, r, c), lambda b, t: (b, 0, 0))
    kernel = functools.partial(_mixer_kernel, rows=tm, mask_initial_keys=mask_initial_keys)
    return pl.pallas_call(
        kernel,
        grid=(batch, nt),
        in_specs=[pl.BlockSpec(memory_space=pltpu.SMEM),
                  row(D_MODEL), row(ATTN_WIDTH), row(KV_WIDTH), row(KV_WIDTH), slab, slab,
                  per_batch(WINDOW, KV_WIDTH), per_batch(WINDOW, KV_WIDTH),
                  per_batch(V7X_SUBLANES, LRU_WIDTH), per_batch(1, LRU_WIDTH),
                  _resident((CONV_WIDTH, LRU_WIDTH)), _resident((1, LRU_WIDTH)),
                  _resident((LRU_WIDTH, 2 * LRU_WIDTH)), _resident((1, LRU_WIDTH)), _resident((1, LRU_WIDTH)),
                  _resident((1, LRU_WIDTH)),
                  _resident((D_MODEL, D_MODEL)), _resident((1, D_MODEL)), _resident((1, D_MODEL))],
        out_specs=[row(D_MODEL), per_batch(1, LRU_WIDTH)],
        out_shape=[jax.ShapeDtypeStruct((batch * seq, D_MODEL), _F32),
                   jax.ShapeDtypeStruct((batch, 1, LRU_WIDTH), _F32)],
        scratch_shapes=[pltpu.VMEM((WINDOW + tm, KV_WIDTH), _BF16), pltpu.VMEM((WINDOW + tm, KV_WIDTH), _BF16),
                        pltpu.VMEM((LRU_SLABS, V7X_SUBLANES + tm, V7X_LANES), _F32),
                        pltpu.VMEM((2, LRU_SLABS, groups, V7X_LANES), _F32),
                        pltpu.VMEM((LRU_SLABS, groups, V7X_LANES), _F32),
                        pltpu.VMEM((LRU_SLABS, tm, V7X_LANES), _F32),
                        pltpu.VMEM((1, LRU_WIDTH), _F32)],
        compiler_params=pltpu.CompilerParams(dimension_semantics=("arbitrary", "arbitrary"),
                                             vmem_limit_bytes=_vmem_limit()),
        name="mixer",
    )(w["attn_sinks"], h1a, q, k, v, xb, gb, k0, v0, conv0, hst0,
      w["conv_w"], w["conv_b"], w["lru_wax"], w["lru_ba"], w["lru_bx"], w["lru_lambda"],
      w["w_out"], w["ln2_g"], w["ln2_b"])


def _ffn_out(h2d, p2d, w):
    n = h2d.shape[0]
    tm = _row_tile(n)
    row = lambda width: pl.BlockSpec((tm, width), lambda i: (i, 0))
    return pl.pallas_call(
        _ffn_out_kernel,
        grid=(n // tm,),
        in_specs=[row(D_MODEL), row(PLE_DIM),
                  _resident((D_MODEL, D_FF)), _resident((D_MODEL, D_FF)), _resident((D_FF, D_MODEL)),
                  _resident((1, D_MODEL)), _resident((1, D_MODEL)),
                  _resident((D_MODEL, D_MODEL)), _resident((PLE_DIM, D_MODEL))],
        out_specs=row(D_MODEL),
        out_shape=jax.ShapeDtypeStruct((n, D_MODEL), _F32),
        compiler_params=pltpu.CompilerParams(dimension_semantics=("arbitrary",),
                                             vmem_limit_bytes=_vmem_limit()),
        name="ffn_out",
    )(h2d, p2d, w["ffn2_wg"], w["ffn2_wu"], w["ffn2_wd"], w["ln3_g"], w["ln3_b"], w["w_ple_gate"], w["w_ple"])


def _rope_tables(pos):
    half = ROT_DIM // 2
    f32 = np.float32
    inv = np.power(f32(ROPE_THETA), -np.arange(half, dtype=f32) * f32(2.0 / ROT_DIM)).astype(f32)
    ang = (pos.astype(f32)[:, None] * inv[None, :]).astype(f32)
    cos, sin = np.cos(ang), np.sin(ang)
    ident = np.ones((pos.shape[0], HEAD_DIM - ROT_DIM), f32)
    cos_head = np.concatenate([cos, cos, ident], axis=1)
    sin_head = np.concatenate([-sin, sin, 0.0 * ident], axis=1)
    reps = V7X_LANES // HEAD_DIM
    return np.tile(cos_head, (1, reps)), np.tile(sin_head, (1, reps))


def _prepare_weights(i, ffn1_wg, ffn1_wu, ffn1_wd, ln1_g, ln1_b, w_in, attn_sinks, conv_w, conv_b,
                     lru_wa, lru_ba, lru_wx, lru_bx, lru_lambda, w_out, ln2_g, ln2_b,
                     ffn2_wg, ffn2_wu, ffn2_wd, ln3_g, ln3_b, w_ple, w_ple_gate):
    row = lambda a: a[i].reshape(1, -1)
    slots = jnp.asarray(_HEAD_SLOTS)
    win = w_in[i]
    wq = win[:, :ATTN_WIDTH].reshape(D_MODEL, N_HEADS, HEAD_DIM)[:, slots].reshape(D_MODEL, ATTN_WIDTH)
    wq = wq * (HEAD_DIM ** -0.5)
    wout = w_out[i]
    wo_attn = wout[:ATTN_WIDTH].reshape(N_HEADS, HEAD_DIM, D_MODEL)[slots].reshape(ATTN_WIDTH, D_MODEL)
    eye = jnp.eye(LRU_BLOCKS, dtype=_F32)
    blockdiag = lambda wb: jnp.einsum("ncd,nm->ncmd", wb, eye).reshape(LRU_WIDTH, LRU_WIDTH)
    return dict(
        ffn1_wg=ffn1_wg[i].astype(_BF16), ffn1_wu=ffn1_wu[i].astype(_BF16), ffn1_wd=ffn1_wd[i].astype(_BF16),
        ln1_g=row(ln1_g), ln1_b=row(ln1_b),
        w_in=jnp.concatenate([wq, win[:, ATTN_WIDTH:]], axis=1).astype(_BF16),
        attn_sinks=attn_sinks[i],
        conv_w=conv_w[i], conv_b=row(conv_b),
        lru_wax=jnp.concatenate([blockdiag(lru_wa[i]), blockdiag(lru_wx[i])], axis=1).astype(_BF16),
        lru_ba=row(lru_ba), lru_bx=row(lru_bx), lru_lambda=row(lru_lambda),
        w_out=jnp.concatenate([wo_attn, wout[ATTN_WIDTH:]], axis=0).astype(_BF16),
        ln2_g=row(ln2_g), ln2_b=row(ln2_b),
        ffn2_wg=ffn2_wg[i].astype(_BF16), ffn2_wu=ffn2_wu[i].astype(_BF16), ffn2_wd=ffn2_wd[i].astype(_BF16),
        ln3_g=row(ln3_g), ln3_b=row(ln3_b),
        w_ple=w_ple[i].astype(_BF16), w_ple_gate=w_ple_gate[i].astype(_BF16))


def _layer(x, p, pos, k_past, v_past, conv_prev, h_prev, w):
    batch, seq = x.shape[0], x.shape[1]
    n = batch * seq
    tm = _row_tile(n)
    cos_tab, sin_tab = _rope_tables(pos)
    if seq < tm:
        cos_tab = np.tile(cos_tab, (tm // seq, 1))
        sin_tab = np.tile(sin_tab, (tm // seq, 1))
    h1a, q, k, v, xb, gb = _ffn_in(x.reshape(n, D_MODEL), jnp.asarray(cos_tab), jnp.asarray(sin_tab), w)
    mask_initial_keys = k_past is None
    if mask_initial_keys:
        k0 = jnp.zeros((batch, WINDOW, KV_WIDTH), _F32)
        v0 = jnp.zeros((batch, WINDOW, KV_WIDTH), _F32)
    else:
        k0 = k_past.reshape(batch, WINDOW, KV_WIDTH)
        v0 = v_past.reshape(batch, WINDOW, KV_WIDTH)
    conv0 = jnp.pad(conv_prev, ((0, 0), (V7X_SUBLANES - (CONV_WIDTH - 1), 0), (0, 0)))
    hst0 = h_prev.reshape(batch, 1, LRU_WIDTH)
    h2, new_h = _mixer(h1a, q, k, v, xb, gb, k0, v0, conv0, hst0, w,
                       batch=batch, seq=seq, mask_initial_keys=mask_initial_keys)
    y = _ffn_out(h2, p.reshape(n, PLE_DIM), w)
    k_all = jnp.concatenate([k0, k.reshape(batch, seq, KV_WIDTH)], axis=1)[:, -WINDOW:]
    v_all = jnp.concatenate([v0, v.reshape(batch, seq, KV_WIDTH)], axis=1)[:, -WINDOW:]
    xb_tail = xb.reshape(LRU_SLABS, batch, seq, V7X_LANES)[:, :, -(CONV_WIDTH - 1):]
    xb_tail = xb_tail.transpose(1, 2, 0, 3).reshape(batch, -1, LRU_WIDTH)
    conv_all = jnp.concatenate([conv_prev, xb_tail], axis=1)[:, -(CONV_WIDTH - 1):]
    return (y.reshape(batch, seq, D_MODEL),
            k_all.reshape(batch, WINDOW, N_KV_HEADS, HEAD_DIM),
            v_all.reshape(batch, WINDOW, N_KV_HEADS, HEAD_DIM),
            conv_all, new_h.reshape(batch, LRU_WIDTH))


def kernel(x_prompt, x_sample, p_prompt, p_sample, cache_k, cache_v, state_conv, state_h, ffn1_wg, ffn1_wu, ffn1_wd, ln1_g, ln1_b, w_in, attn_sinks, conv_w, conv_b, lru_wa, lru_ba, lru_wx, lru_bx, lru_lambda, w_out, ln2_g, ln2_b, ffn2_wg, ffn2_wu, ffn2_wd, ln3_g, ln3_b, w_ple, w_ple_gate):
    pos_prompt = np.arange(x_prompt.shape[1])
    pos_sample = PAST_LEN + np.arange(x_sample.shape[1])
    yp, ys = x_prompt, x_sample
    outs_p, outs_s = [], []
    for i in range(DEPTH):
        w = _prepare_weights(i, ffn1_wg, ffn1_wu, ffn1_wd, ln1_g, ln1_b, w_in, attn_sinks, conv_w, conv_b,
                             lru_wa, lru_ba, lru_wx, lru_bx, lru_lambda, w_out, ln2_g, ln2_b,
                             ffn2_wg, ffn2_wu, ffn2_wd, ln3_g, ln3_b, w_ple, w_ple_gate)
        bp = yp.shape[0]
        conv_zero = jnp.zeros((bp, CONV_WIDTH - 1, LRU_WIDTH), yp.dtype)
        h_zero = jnp.zeros((bp, LRU_WIDTH), _F32)
        yp, *rest_p = _layer(yp, p_prompt[i], pos_prompt, None, None, conv_zero, h_zero, w)
        ys, *rest_s = _layer(ys, p_sample[i], pos_sample, cache_k[i], cache_v[i], state_conv[i], state_h[i], w)
        outs_p.append(rest_p)
        outs_s.append(rest_s)
    stack = lambda outs, j: jnp.stack([o[j] for o in outs])
    return (yp, ys,
            stack(outs_p, 0), stack(outs_p, 1), stack(outs_p, 2), stack(outs_p, 3),
            stack(outs_s, 0), stack(outs_s, 1), stack(outs_s, 2), stack(outs_s, 3))
```
